```python
import math
import jax, jax.numpy as jnp
from jax import lax
import numpy as np


D_MODEL = 1024
BATCH = 16
SEQ = 4096
DEPTH = 4

CHUNK = 64
Q_BLOCK = 128
ATT_HEADS = 4
ATT_HEAD_DIM = 64
ATT_V_DIM = 2 * ATT_HEAD_DIM
ATT_WIDTH = ATT_HEADS * ATT_V_DIM
QK_COLS = ATT_HEADS * 2 * ATT_HEAD_DIM
ROPE_THETA = 10000.0
CONV_CH = 512
CONV_WIDTH = 31
GMLP_CH = 512
GMLP_GROUPS = 4
GMLP_GROUP_CH = GMLP_CH // GMLP_GROUPS
GMLP_BLOCK = 128
N_BRANCH = 3
FFN_DIM = 2816
FFN_CONV_WIDTH = 3
EPS = 1e-6
SPLIT_POINTS = (QK_COLS, 2 * QK_COLS, 2 * QK_COLS + ATT_WIDTH,
                2 * QK_COLS + ATT_WIDTH + 2 * CONV_CH,
                2 * QK_COLS + ATT_WIDTH + 2 * CONV_CH + 2 * GMLP_CH)
IN_COLS = 2 * QK_COLS + ATT_WIDTH + 2 * CONV_CH + 2 * GMLP_CH + N_BRANCH * D_MODEL

kernel_name = 'hybrid_diffattn_conformer_gmlp_convffn_adaln'


def rms_norm(x, g):
    x32 = x.astype(jnp.float32)
    y = x32 * lax.rsqrt(jnp.mean(x32 * x32, axis=-1, keepdims=True) + EPS)
    return (y * g.astype(jnp.float32)).astype(x.dtype)


def layer_norm(x, g, b):
    x32 = x.astype(jnp.float32)
    mu = jnp.mean(x32, axis=-1, keepdims=True)
    var = jnp.mean(jnp.square(x32 - mu), axis=-1, keepdims=True)
    y = (x32 - mu) * lax.rsqrt(var + EPS) * g.astype(jnp.float32) + b.astype(jnp.float32)
    return y.astype(x.dtype)


def causal_depthwise_conv(x, w, b):
    k = w.shape[0]
    xp = jnp.pad(x, ((0, 0), (k - 1, 0), (0, 0)))
    y = lax.conv_general_dilated(xp, w[:, None, :].astype(x.dtype), window_strides=(1,), padding='VALID',
                                 dimension_numbers=('NWC', 'WIO', 'NWC'),
                                 feature_group_count=x.shape[-1])
    return y + b


def apply_rope(x, cos, sin):
    half = x.shape[-1] // 2
    x1, x2 = x[..., :half], x[..., half:]
    return jnp.concatenate([x1 * cos - x2 * sin, x2 * cos + x1 * sin], axis=-1)


def diff_attention(q, k, v, lam):
    B, S = q.shape[0], q.shape[1]
    nb = S // Q_BLOCK
    qb = q.reshape(B, nb, Q_BLOCK, ATT_HEADS, 2, ATT_HEAD_DIM).transpose(1, 0, 2, 3, 4, 5)
    k_chunk = jnp.arange(S) // CHUNK

    def one_block(args):
        qi, bi = args
        q_chunk = (bi * Q_BLOCK + jnp.arange(Q_BLOCK)) // CHUNK
        mask = q_chunk[:, None] >= k_chunk[None, :]
        s = jnp.einsum('bqhmd,bkhmd->bhmqk', qi, k).astype(jnp.float32)
        s = jnp.where(mask, s, -1e30)
        p = jax.nn.softmax(s, axis=-1)
        a = (p[:, :, 0] - lam * p[:, :, 1]).astype(v.dtype)
        return jnp.einsum('bhqk,bkhe->bqhe', a, v)

    out = lax.map(one_block, (qb, jnp.arange(nb)))
    return out.transpose(1, 0, 2, 3, 4).reshape(B, S, ATT_HEADS, ATT_V_DIM)


def setup_inputs(seed: int = 0) -> dict:
    key = jax.random.key(seed)
    ks = jax.random.split(key, 32)
    L, D, F = DEPTH, D_MODEL, FFN_DIM

    def nrm(k, shape, scale):
        return jax.random.normal(k, shape, dtype=jnp.float32) * scale

    offset = jax.random.randint(ks[2], (BATCH, 1), 0, SEQ, dtype=jnp.int32)
    positions = offset + jnp.arange(SEQ, dtype=jnp.int32)[None, :]
    return {
        'x': nrm(ks[0], (BATCH, SEQ, D), 1.0),
        'c': nrm(ks[1], (BATCH, D), 1.0),
        'positions': positions,
        'ln1_g': 1.0 + nrm(ks[3], (L, D), 0.02),
        'ln2_g': 1.0 + nrm(ks[4], (L, D), 0.02),
        'w_ada': nrm(ks[5], (L, D, 6 * D), 0.5 * D ** -0.5),
        'b_ada': nrm(ks[6], (L, 6 * D), 0.02),
        'w_in': nrm(ks[7], (L, D, IN_COLS), D ** -0.5),
        'b_in': nrm(ks[8], (L, IN_COLS), 0.02),
        'lambda_q1': nrm(ks[9], (L, ATT_HEAD_DIM), 0.1),
        'lambda_k1': nrm(ks[10], (L, ATT_HEAD_DIM), 0.1),
        'lambda_q2': nrm(ks[11], (L, ATT_HEAD_DIM), 0.1),
        'lambda_k2': nrm(ks[12], (L, ATT_HEAD_DIM), 0.1),
        'attn_subln_g': 1.0 + nrm(ks[13], (L, ATT_V_DIM), 0.02),
        'w_attn_out': nrm(ks[14], (L, ATT_WIDTH, D), ATT_WIDTH ** -0.5),
        'conv_dw_w': nrm(ks[15], (L, CONV_WIDTH, CONV_CH), CONV_WIDTH ** -0.5),
        'conv_dw_b': nrm(ks[16], (L, CONV_CH), 0.02),
        'conv_ln_g': 1.0 + nrm(ks[17], (L, CONV_CH), 0.02),
        'conv_ln_b': nrm(ks[18], (L, CONV_CH), 0.02),
        'w_conv_out': nrm(ks[19], (L, CONV_CH, D), CONV_CH ** -0.5),
        'gmlp_ln_g': 1.0 + nrm(ks[20], (L, GMLP_CH), 0.02),
        'gmlp_ln_b': nrm(ks[21], (L, GMLP_CH), 0.02),
        'w_spatial': nrm(ks[22], (L, GMLP_GROUPS, GMLP_BLOCK, GMLP_BLOCK), GMLP_BLOCK ** -0.5),
        'b_spatial': 1.0 + nrm(ks[23], (L, GMLP_GROUPS, GMLP_BLOCK), 0.02),
        'w_gmlp_out': nrm(ks[24], (L, GMLP_CH, D), GMLP_CH ** -0.5),
        'w_o': nrm(ks[25], (L, D, D), D ** -0.5),
        'w_up': nrm(ks[26], (L, D, 2 * F), D ** -0.5),
        'ffn_dw_w': nrm(ks[27], (L, FFN_CONV_WIDTH, 2 * F), FFN_CONV_WIDTH ** -0.5),
        'ffn_dw_b': nrm(ks[28], (L, 2 * F), 0.02),
        'w_down': nrm(ks[29], (L, F, D), F ** -0.5),
        'final_g': 1.0 + nrm(ks[30], (D,), 0.02),
    }


def reference(x, c, positions, ln1_g, ln2_g, w_ada, b_ada, w_in, b_in,
              lambda_q1, lambda_k1, lambda_q2, lambda_k2, attn_subln_g, w_attn_out,
              conv_dw_w, conv_dw_b, conv_ln_g, conv_ln_b, w_conv_out,
              gmlp_ln_g, gmlp_ln_b, w_spatial, b_spatial, w_gmlp_out, w_o,
              w_up, ffn_dw_w, ffn_dw_b, w_down, final_g):
    B, S, D = x.shape
    inv_freq = 1.0 / (ROPE_THETA ** (jnp.arange(0, ATT_HEAD_DIM, 2, dtype=jnp.float32) / ATT_HEAD_DIM))
    ang = positions.astype(jnp.float32)[..., None] * inv_freq
    cos = jnp.cos(ang)[:, :, None, None, :].astype(x.dtype)
    sin = jnp.sin(ang)[:, :, None, None, :].astype(x.dtype)
    tri = jnp.tril(jnp.ones((GMLP_BLOCK, GMLP_BLOCK), dtype=bool))
    c_act = jax.nn.silu(c)

    for l in range(DEPTH):
        mod = c_act @ w_ada[l] + b_ada[l]
        sh1, sc1, g1, sh2, sc2, g2 = [m[:, None, :] for m in jnp.split(mod, 6, axis=-1)]

        h = rms_norm(x, ln1_g[l]) * (1.0 + sc1) + sh1
        cols = h @ w_in[l] + b_in[l]
        q, k, v, conv_in, gmlp_in, gate_in = jnp.split(cols, SPLIT_POINTS, axis=-1)

        q = apply_rope(q.reshape(B, S, ATT_HEADS, 2, ATT_HEAD_DIM), cos, sin) * (ATT_HEAD_DIM ** -0.5)
        k = apply_rope(k.reshape(B, S, ATT_HEADS, 2, ATT_HEAD_DIM), cos, sin)
        v = v.reshape(B, S, ATT_HEADS, ATT_V_DIM)
        lam_init = 0.8 - 0.6 * math.exp(-0.3 * l)
        lam = (jnp.exp(jnp.sum(lambda_q1[l].astype(jnp.float32) * lambda_k1[l].astype(jnp.float32)))
               - jnp.exp(jnp.sum(lambda_q2[l].astype(jnp.float32) * lambda_k2[l].astype(jnp.float32)))
               + lam_init)
        o = diff_attention(q, k, v, lam)
        o = rms_norm(o, attn_subln_g[l]) * (1.0 - lam_init)
        y_att = o.reshape(B, S, ATT_WIDTH) @ w_attn_out[l]

        a, gt = jnp.split(conv_in, 2, axis=-1)
        hc = a * jax.nn.sigmoid(gt)
        hc = causal_depthwise_conv(hc, conv_dw_w[l], conv_dw_b[l])
        hc = jax.nn.silu(layer_norm(hc, conv_ln_g[l], conv_ln_b[l]))
        y_conv = hc @ w_conv_out[l]

        z = jax.nn.gelu(gmlp_in)
        u, vv = jnp.split(z, 2, axis=-1)
        vv = layer_norm(vv, gmlp_ln_g[l], gmlp_ln_b[l])
        vv = vv.reshape(B, S // GMLP_BLOCK, GMLP_BLOCK, GMLP_GROUPS, GMLP_GROUP_CH)
        ws = jnp.where(tri, w_spatial[l], 0.0)
        sg = jnp.einsum('gqp,bnpgc->bnqgc', ws, vv) + b_spatial[l].T[:, :, None]
        y_gmlp = (u * sg.reshape(B, S, GMLP_CH)) @ w_gmlp_out[l]

        ga, gc, gm = jnp.split(jax.nn.sigmoid(gate_in), N_BRANCH, axis=-1)
        mixed = ga * y_att + gc * y_conv + gm * y_gmlp
        x = x + g1 * (mixed @ w_o[l])

        h2 = rms_norm(x, ln2_g[l]) * (1.0 + sc2) + sh2
        up = causal_depthwise_conv(h2 @ w_up[l], ffn_dw_w[l], ffn_dw_b[l])
        ff_gate, ff_val = jnp.split(up, 2, axis=-1)
        x = x + g2 * ((jax.nn.silu(ff_gate) * ff_val) @ w_down[l])

    return rms_norm(x, final_g)
```

```python
import functools
import math

import jax
import jax.numpy as jnp
from jax import lax
from jax.experimental import pallas as pl
from jax.experimental.pallas import tpu as pltpu

D_MODEL = 1024
CHUNK = 64
ATT_HEADS = 4
ATT_HEAD_DIM = 64
ATT_V_DIM = 128
ATT_WIDTH = 512
QK_COLS = 512
ROPE_THETA = 10000.0
CONV_CH = 512
CONV_WIDTH = 31
GMLP_CH = 512
GMLP_GROUPS = 4
GMLP_BLOCK = 128
FFN_DIM = 2816
FFN_CONV_WIDTH = 3
EPS = 1e-6
QKV_COLS = 2 * QK_COLS + ATT_WIDTH
CONV_IN_COLS = 2 * CONV_CH
GMLP_IN_COLS = 2 * GMLP_CH
BRANCH_COLS = QKV_COLS + CONV_IN_COLS + GMLP_IN_COLS
GATE_COLS = 3 * D_MODEL

LANES = 128
SUBLANES = 8
CONV_HALO = 32
VMEM_LIMIT = 56 * 1024 * 1024
NEG_INF = -1e30
CHUNK_SHIFT = CHUNK.bit_length() - 1
TOKEN_TILE = 512
ATTN_TILE = 512
FFN_CHUNK = 256

F32 = jnp.float32
BF16 = jnp.bfloat16


def _const_spec(shape):
    n = len(shape)
    return pl.BlockSpec(shape, lambda *_: (0,) * n, pipeline_mode=pl.Buffered(1))


def _params(sem):
    return pltpu.CompilerParams(dimension_semantics=sem, vmem_limit_bytes=VMEM_LIMIT)


def _ada_norm(x, g, scale, shift):
    r = lax.rsqrt(jnp.mean(x * x, axis=-1, keepdims=True) + EPS)
    return x * r * (g * (1.0 + scale)) + shift


def _mod_kernel(c_ref, w_ref, b_ref, o_ref):
    c = c_ref[...]
    ca = (c * jax.nn.sigmoid(c)).astype(BF16)
    o_ref[...] = jnp.dot(ca, w_ref[...].astype(BF16), preferred_element_type=F32) + b_ref[...]


def _modulation(c, w_ada, b_ada):
    L, D, N = w_ada.shape
    B = c.shape[0]
    tn = 1536
    return pl.pallas_call(
        _mod_kernel,
        grid=(L, N // tn),
        in_specs=[pl.BlockSpec((B, D), lambda l, j: (0, 0)),
                  pl.BlockSpec((None, D, tn), lambda l, j: (l, 0, j)),
                  pl.BlockSpec((None, 1, tn), lambda l, j: (l, 0, j))],
        out_specs=pl.BlockSpec((None, B, tn), lambda l, j: (l, 0, j)),
        out_shape=jax.ShapeDtypeStruct((L, B, N), F32),
        compiler_params=_params(("parallel", "parallel")),
        name="adaln_mod",
    )(c, w_ada, b_ada.reshape(L, 1, N))


def _proj_kernel(x_ref, mod_ref, g_ref, w_ref, b_ref, cos_ref, sin_ref, wsp_ref, bsp_ref,
                 lng_ref, lnb_ref, q_ref, k_ref, v_ref, hc_ref, tg_ref):
    tm = x_ref.shape[0]
    h = _ada_norm(x_ref[...], g_ref[...], mod_ref[1:2, :], mod_ref[0:1, :]).astype(BF16)

    qkv = jnp.dot(h, w_ref[:, 0:QKV_COLS], preferred_element_type=F32) + b_ref[:, 0:QKV_COLS]
    cos = jnp.concatenate([cos_ref[...]] * (QK_COLS // LANES), axis=1)
    sin = jnp.concatenate([sin_ref[...]] * (QK_COLS // LANES), axis=1)
    lane = lax.broadcasted_iota(jnp.int32, (tm, QK_COLS), 1)
    first_half = (lane & (ATT_HEAD_DIM - 1)) < (ATT_HEAD_DIM // 2)

    def rope(t):
        fwd = pltpu.roll(t, QK_COLS - ATT_HEAD_DIM // 2, axis=1)
        bwd = pltpu.roll(t, ATT_HEAD_DIM // 2, axis=1)
        return t * cos + jnp.where(first_half, fwd, bwd) * sin

    q_ref[...] = (rope(qkv[:, 0:QK_COLS]) * (ATT_HEAD_DIM ** -0.5)).astype(BF16)
    k_ref[...] = rope(qkv[:, QK_COLS:2 * QK_COLS]).astype(BF16)
    v_ref[...] = qkv[:, 2 * QK_COLS:QKV_COLS].astype(BF16)

    c0 = QKV_COLS
    cv = jnp.dot(h, w_ref[:, c0:c0 + CONV_IN_COLS], preferred_element_type=F32) + b_ref[:, c0:c0 + CONV_IN_COLS]
    hc_ref[...] = (cv[:, 0:CONV_CH] * jax.nn.sigmoid(cv[:, CONV_CH:])).astype(BF16)

    c1 = c0 + CONV_IN_COLS
    z = jax.nn.gelu(jnp.dot(h, w_ref[:, c1:c1 + GMLP_IN_COLS], preferred_element_type=F32)
                    + b_ref[:, c1:c1 + GMLP_IN_COLS])
    u = z[:, 0:GMLP_CH]
    vv = z[:, GMLP_CH:]
    mu = jnp.mean(vv, axis=-1, keepdims=True)
    vc = vv - mu
    var = jnp.mean(vc * vc, axis=-1, keepdims=True)
    vn = (vc * lax.rsqrt(var + EPS) * lng_ref[...] + lnb_ref[...]).astype(BF16)
    row = lax.broadcasted_iota(jnp.int32, (GMLP_BLOCK, GMLP_BLOCK), 0)
    col = lax.broadcasted_iota(jnp.int32, (GMLP_BLOCK, GMLP_BLOCK), 1)
    gc = GMLP_CH // GMLP_GROUPS
    for g in range(GMLP_GROUPS):
        ws = jnp.where(row >= col, wsp_ref[g], 0.0).astype(BF16)
        bias = bsp_ref[g]
        for n in range(tm // GMLP_BLOCK):
            rs = slice(n * GMLP_BLOCK, (n + 1) * GMLP_BLOCK)
            cs = slice(g * gc, (g + 1) * gc)
            sg = jnp.dot(ws, vn[rs, cs], preferred_element_type=F32) + bias
            tg_ref[rs, cs] = (u[rs, cs] * sg).astype(BF16)


def _proj(x, mod, ln_g, w, b, cos_t, sin_t, w_sp, b_sp_t, gln_g, gln_b, tm):
    B, S, D = x.shape
    tok = lambda n: pl.BlockSpec((None, tm, n), lambda bi, i: (bi, i, 0))
    outs = [jax.ShapeDtypeStruct((B, S, 512), BF16)] * 5
    return pl.pallas_call(
        _proj_kernel,
        grid=(B, S // tm),
        in_specs=[tok(D),
                  pl.BlockSpec((None, 6, D), lambda bi, i: (bi, 0, 0)),
                  _const_spec((1, D)),
                  _const_spec((D, BRANCH_COLS)),
                  _const_spec((1, BRANCH_COLS)),
                  tok(LANES), tok(LANES),
                  _const_spec((GMLP_GROUPS, GMLP_BLOCK, GMLP_BLOCK)),
                  _const_spec((GMLP_GROUPS, GMLP_BLOCK, GMLP_BLOCK)),
                  _const_spec((1, GMLP_CH)), _const_spec((1, GMLP_CH))],
        out_specs=[tok(512)] * 5,
        out_shape=outs,
        compiler_params=_params(("parallel", "parallel")),
        name="in_proj",
    )(x, mod, ln_g, w, b, cos_t, sin_t, w_sp, b_sp_t, gln_g, gln_b)


def _attn_kernel(lq1_ref, lk1_ref, lq2_ref, lk2_ref, q_ref, k_ref, v_ref, g_ref, o_ref,
                 m_ref, l_ref, acc_ref, *, t, lam_init):
    i = pl.program_id(2)
    q = q_ref[...]
    lane = lax.broadcasted_iota(jnp.int32, q.shape, 1)
    zero = jnp.zeros_like(q)
    qs = jnp.concatenate([jnp.where(lane < ATT_HEAD_DIM, q, zero),
                          jnp.where(lane >= ATT_HEAD_DIM, q, zero)], axis=0)
    m_ref[...] = jnp.full_like(m_ref, NEG_INF)
    l_ref[...] = jnp.zeros_like(l_ref)
    acc_ref[...] = jnp.zeros_like(acc_ref)

    def step(j, masked):
        off = pl.multiple_of(j * t, t)
        kj = k_ref[pl.ds(off, t), :]
        vj = v_ref[pl.ds(off, t), :]
        s = lax.dot_general(qs, kj, (((1,), (1,)), ((), ())), preferred_element_type=F32)
        if masked:
            rq = lax.broadcasted_iota(jnp.int32, (t, t), 0)
            ck = lax.broadcasted_iota(jnp.int32, (t, t), 1)
            keep = (rq >> CHUNK_SHIFT) >= (ck >> CHUNK_SHIFT)
            s = jnp.where(jnp.concatenate([keep, keep], axis=0), s, NEG_INF)
        m_old = m_ref[...]
        m_new = jnp.maximum(m_old, jnp.max(s, axis=-1, keepdims=True))
        alpha = jnp.exp(m_old - m_new)
        p = jnp.exp(s - m_new)
        l_ref[...] = alpha * l_ref[...] + jnp.sum(p, axis=-1, keepdims=True)
        acc_ref[...] = alpha * acc_ref[...] + jnp.dot(p.astype(BF16), vj, preferred_element_type=F32)
        m_ref[...] = m_new

    def body(j, carry):
        step(j, False)
        return carry

    lax.fori_loop(0, i, body, 0)
    step(i, True)

    lam = (jnp.exp(jnp.sum(lq1_ref[...] * lk1_ref[...], axis=-1, keepdims=True))
           - jnp.exp(jnp.sum(lq2_ref[...] * lk2_ref[...], axis=-1, keepdims=True)) + lam_init)
    o_all = acc_ref[...] / l_ref[...]
    o = o_all[0:t, :] - lam * o_all[t:2 * t, :]
    r = lax.rsqrt(jnp.mean(o * o, axis=-1, keepdims=True) + EPS)
    o_ref[...] = (o * r * g_ref[...] * (1.0 - lam_init)).astype(BF16)


def _attention(q, k, v, lq1, lk1, lq2, lk2, subln_g, lam_init, t):
    B, S, _ = q.shape
    lam_spec = _const_spec((1, ATT_HEAD_DIM))
    kv_spec = pl.BlockSpec((None, S, ATT_V_DIM), lambda b, h, i: (b, 0, h))
    qo_spec = pl.BlockSpec((None, t, ATT_V_DIM), lambda b, h, i: (b, i, h))
    return pl.pallas_call(
        functools.partial(_attn_kernel, t=t, lam_init=lam_init),
        grid=(B, ATT_HEADS, S // t),
        in_specs=[lam_spec, lam_spec, lam_spec, lam_spec, qo_spec, kv_spec, kv_spec,
                  _const_spec((1, ATT_V_DIM))],
        out_specs=qo_spec,
        out_shape=jax.ShapeDtypeStruct((B, S, ATT_WIDTH), BF16),
        scratch_shapes=[pltpu.VMEM((2 * t, 1), F32), pltpu.VMEM((2 * t, 1), F32),
                        pltpu.VMEM((2 * t, ATT_V_DIM), F32)],
        compiler_params=_params(("parallel", "parallel", "parallel")),
        name="diff_attn",
    )(lq1, lk1, lq2, lk2, q, k, v, subln_g)


def _merge_kernel(x_ref, mod_ref, g_ref, wg_ref, bg_ref, o_ref, wa_ref, hcp_ref, hc_ref, cw_ref, cb_ref,
                  clg_ref, clb_ref, wc_ref, tg_ref, wm_ref, wo_ref, y_ref, ext_ref, cn_ref):
    tm = x_ref.shape[0]
    i = pl.program_id(1)
    x = x_ref[...]
    h = _ada_norm(x, g_ref[...], mod_ref[1:2, :], mod_ref[0:1, :]).astype(BF16)
    gates = jax.nn.sigmoid(jnp.dot(h, wg_ref[...], preferred_element_type=F32) + bg_ref[...])

    y_att = jnp.dot(o_ref[...], wa_ref[...], preferred_element_type=F32)
    mixed = gates[:, 0:D_MODEL] * y_att

    prev = hcp_ref[...].astype(F32)
    ext_ref[0, 0:CONV_HALO, :] = jnp.where(i > 0, prev, jnp.zeros_like(prev))
    ext_ref[0, CONV_HALO:, :] = hc_ref[...].astype(F32)
    n_sh = tm + CONV_HALO - SUBLANES
    for s in range(1, SUBLANES):
        ext_ref[s, 0:n_sh, :] = ext_ref[0, s:s + n_sh, :]
    rb = 32
    first = CONV_HALO - (CONV_WIDTH - 1)

    def conv_rows(r, carry):
        base = pl.multiple_of(r * rb, rb)
        acc = jnp.zeros((rb, CONV_CH), F32) + cb_ref[...]
        for j in range(CONV_WIDTH):
            s = (first + j) % SUBLANES
            acc = acc + ext_ref[s, pl.ds(base + (first + j - s), rb), :] * cw_ref[j:j + 1, :]
        mu = jnp.mean(acc, axis=-1, keepdims=True)
        ac = acc - mu
        var = jnp.mean(ac * ac, axis=-1, keepdims=True)
        yn = ac * lax.rsqrt(var + EPS) * clg_ref[...] + clb_ref[...]
        cn_ref[pl.ds(base, rb), :] = (yn * jax.nn.sigmoid(yn)).astype(BF16)
        return carry

    lax.fori_loop(0, tm // rb, conv_rows, 0)
    y_conv = jnp.dot(cn_ref[...], wc_ref[...], preferred_element_type=F32)
    mixed = mixed + gates[:, D_MODEL:2 * D_MODEL] * y_conv

    y_gmlp = jnp.dot(tg_ref[...], wm_ref[...], preferred_element_type=F32)
    mixed = mixed + gates[:, 2 * D_MODEL:] * y_gmlp

    y_ref[...] = x + mod_ref[2:3, :] * jnp.dot(mixed.astype(BF16), wo_ref[...], preferred_element_type=F32)


def _merge(x, mod, ln_g, w_gate, b_gate, o, w_att, hc, cw, cb, clg, clb, w_conv, tg, w_gmlp, w_o, tm):
    B, S, D = x.shape
    tok = lambda n: pl.BlockSpec((None, tm, n), lambda bi, i: (bi, i, 0))
    halo = pl.BlockSpec((None, CONV_HALO, CONV_CH),
                        lambda bi, i: (bi, jnp.maximum(i * (tm // CONV_HALO) - 1, 0), 0))
    return pl.pallas_call(
        _merge_kernel,
        grid=(B, S // tm),
        in_specs=[tok(D),
                  pl.BlockSpec((None, 6, D), lambda bi, i: (bi, 0, 0)),
                  _const_spec((1, D)),
                  _const_spec((D, GATE_COLS)), _const_spec((1, GATE_COLS)),
                  tok(ATT_WIDTH), _const_spec((ATT_WIDTH, D)),
                  halo, tok(CONV_CH),
                  _const_spec((CONV_WIDTH, CONV_CH)), _const_spec((1, CONV_CH)),
                  _const_spec((1, CONV_CH)), _const_spec((1, CONV_CH)),
                  _const_spec((CONV_CH, D)),
                  tok(GMLP_CH), _const_spec((GMLP_CH, D)),
                  _const_spec((D, D))],
        out_specs=tok(D),
        out_shape=jax.ShapeDtypeStruct((B, S, D), F32),
        scratch_shapes=[pltpu.VMEM((SUBLANES, CONV_HALO + tm, CONV_CH), F32), pltpu.VMEM((tm, CONV_CH), BF16)],
        compiler_params=_params(("parallel", "parallel")),
        name="merge",
    )(x, mod, ln_g, w_gate, b_gate, o, w_att, hc, hc, cw, cb, clg, clb, w_conv, tg, w_gmlp, w_o)


def _ffn_kernel(x_ref, mod_ref, g_ref, wu_ref, cw_ref, cb_ref, wd_ref, fg_ref, y_ref,
                h_ref, ext_ref, carry_ref, acc_ref, *, final):
    tm = x_ref.shape[0]
    nc, _, fc2 = wu_ref.shape
    fc = fc2 // 2
    i = pl.program_id(1)
    h_ref[...] = _ada_norm(x_ref[...], g_ref[...], mod_ref[4:5, :], mod_ref[3:4, :]).astype(BF16)
    acc_ref[...] = jnp.zeros_like(acc_ref)

    @pl.when(i == 0)
    def _():
        carry_ref[...] = jnp.zeros_like(carry_ref)

    def chunk(c, carry):
        up = jnp.dot(h_ref[...], wu_ref[c], preferred_element_type=F32)
        ext_ref[0:SUBLANES, :] = carry_ref[c]
        ext_ref[SUBLANES:, :] = up
        carry_ref[c] = up[tm - SUBLANES:, :]
        cw = cw_ref[c]
        out = cb_ref[c] + up * cw[2:3, :]
        out = out + ext_ref[SUBLANES - 1:SUBLANES - 1 + tm, :] * cw[1:2, :]
        out = out + ext_ref[SUBLANES - 2:SUBLANES - 2 + tm, :] * cw[0:1, :]
        gate = out[:, 0:fc]
        act = (gate * jax.nn.sigmoid(gate) * out[:, fc:]).astype(BF16)
        acc_ref[...] += jnp.dot(act, wd_ref[c], preferred_element_type=F32)
        return carry

    lax.fori_loop(0, nc, chunk, 0)
    y = x_ref[...] + mod_ref[5:6, :] * acc_ref[...]
    if final:
        r = lax.rsqrt(jnp.mean(y * y, axis=-1, keepdims=True) + EPS)
        y = y * r * fg_ref[...]
    y_ref[...] = y


def _ffn(x, mod, ln_g, w_up_c, cw_c, cb_c, w_down_c, final_g, tm, final):
    B, S, D = x.shape
    nc, _, fc2 = w_up_c.shape
    tok = pl.BlockSpec((None, tm, D), lambda bi, i: (bi, i, 0))
    return pl.pallas_call(
        functools.partial(_ffn_kernel, final=final),
        grid=(B, S // tm),
        in_specs=[tok,
                  pl.BlockSpec((None, 6, D), lambda bi, i: (bi, 0, 0)),
                  _const_spec((1, D)),
                  _const_spec((nc, D, fc2)),
                  _const_spec((nc, FFN_CONV_WIDTH, fc2)), _const_spec((nc, 1, fc2)),
                  _const_spec((nc, fc2 // 2, D)),
                  _const_spec((1, D))],
        out_specs=tok,
        out_shape=jax.ShapeDtypeStruct((B, S, D), F32),
        scratch_shapes=[pltpu.VMEM((tm, D), BF16), pltpu.VMEM((SUBLANES + tm, fc2), F32),
                        pltpu.VMEM((nc, SUBLANES, fc2), F32), pltpu.VMEM((tm, D), F32)],
        compiler_params=_params(("parallel", "arbitrary")),
        name="ffn",
    )(x, mod, ln_g, w_up_c, cw_c, cb_c, w_down_c, final_g)


def _chunk_cols(a, fc):
    lead = a.shape[:-1]
    g = a[..., :FFN_DIM].reshape(*lead, FFN_DIM // fc, fc)
    v = a[..., FFN_DIM:].reshape(*lead, FFN_DIM // fc, fc)
    return jnp.moveaxis(jnp.concatenate([g, v], axis=-1), -2, 0)


def _tile(s, pref):
    return pref if s % pref == 0 else s


def kernel(x, c, positions, ln1_g, ln2_g, w_ada, b_ada, w_in, b_in, lambda_q1, lambda_k1, lambda_q2, lambda_k2,
           attn_subln_g, w_attn_out, conv_dw_w, conv_dw_b, conv_ln_g, conv_ln_b, w_conv_out, gmlp_ln_g, gmlp_ln_b,
           w_spatial, b_spatial, w_gmlp_out, w_o, w_up, ffn_dw_w, ffn_dw_b, w_down, final_g):
    B, S, D = x.shape
    L = w_in.shape[0]
    tm = _tile(S, TOKEN_TILE)
    ta = _tile(S, ATTN_TILE)

    inv_freq = 1.0 / (ROPE_THETA ** (jnp.arange(0, ATT_HEAD_DIM, 2, dtype=F32) / ATT_HEAD_DIM))
    ang = positions.astype(F32)[..., None] * inv_freq
    cos, sin = jnp.cos(ang), jnp.sin(ang)
    cos_t = jnp.concatenate([cos, cos, cos, cos], axis=-1)
    sin_t = jnp.concatenate([-sin, sin, -sin, sin], axis=-1)

    mod = _modulation(c, w_ada, b_ada).reshape(L, B, 6, D)
    row = lambda a: a.reshape(1, -1)

    for l in range(L):
        lam_init = 0.8 - 0.6 * math.exp(-0.3 * l)
        w_l = w_in[l].astype(BF16)
        q, k, v, hc, tg = _proj(x, mod[l], row(ln1_g[l]), w_l[:, :BRANCH_COLS], row(b_in[l, :BRANCH_COLS]),
                                cos_t, sin_t, w_spatial[l],
                                jnp.broadcast_to(b_spatial[l][:, :, None], w_spatial[l].shape),
                                row(gmlp_ln_g[l]), row(gmlp_ln_b[l]), tm)
        o = _attention(q, k, v, row(lambda_q1[l]), row(lambda_k1[l]), row(lambda_q2[l]), row(lambda_k2[l]),
                       row(attn_subln_g[l]), lam_init, ta)
        x = _merge(x, mod[l], row(ln1_g[l]), w_l[:, BRANCH_COLS:], row(b_in[l, BRANCH_COLS:]),
                   o, w_attn_out[l].astype(BF16), hc, conv_dw_w[l], row(conv_dw_b[l]),
                   row(conv_ln_g[l]), row(conv_ln_b[l]), w_conv_out[l].astype(BF16),
                   tg, w_gmlp_out[l].astype(BF16), w_o[l].astype(BF16), tm)
        x = _ffn(x, mod[l], row(ln2_g[l]), _chunk_cols(w_up[l].astype(BF16), FFN_CHUNK),
                 _chunk_cols(ffn_dw_w[l], FFN_CHUNK), _chunk_cols(row(ffn_dw_b[l]), FFN_CHUNK),
                 w_down[l].astype(BF16).reshape(FFN_DIM // FFN_CHUNK, FFN_CHUNK, D), row(final_g),
                 tm, l == L - 1)
    return x
```

```python
import functools
import math

import jax
import jax.numpy as jnp
from jax import lax
from jax.experimental import pallas as pl
from jax.experimental.pallas import tpu as pltpu

D_MODEL = 1024
CHUNK = 64
ATT_HEADS = 4
ATT_HEAD_DIM = 64
ATT_V_DIM = 128
ATT_WIDTH = 512
QK_COLS = 512
ROPE_THETA = 10000.0
CONV_CH = 512
CONV_WIDTH = 31
GMLP_CH = 512
GMLP_GROUPS = 4
GMLP_BLOCK = 128
FFN_DIM = 2816
FFN_CONV_WIDTH = 3
EPS = 1e-6
QKV_COLS = 2 * QK_COLS + ATT_WIDTH
CONV_IN_COLS = 2 * CONV_CH
GMLP_IN_COLS = 2 * GMLP_CH
BRANCH_COLS = QKV_COLS + CONV_IN_COLS + GMLP_IN_COLS
GATE_COLS = 3 * D_MODEL

LANES = 128
SUBLANES = 8
CONV_HALO = 32
VMEM_LIMIT = 56 * 1024 * 1024
NEG_INF = -1e30
CHUNK_SHIFT = CHUNK.bit_length() - 1
TOKEN_TILE = 512
SUM_ROWS = 16
FFN_CHUNK = 256

F32 = jnp.float32
BF16 = jnp.bfloat16


def _const_spec(shape):
    n = len(shape)
    return pl.BlockSpec(shape, lambda *_: (0,) * n, pipeline_mode=pl.Buffered(1))


def _params(sem):
    return pltpu.CompilerParams(dimension_semantics=sem, vmem_limit_bytes=VMEM_LIMIT)


def _ada_norm(x, g, scale, shift):
    r = lax.rsqrt(jnp.mean(x * x, axis=-1, keepdims=True) + EPS)
    return x * r * (g * (1.0 + scale)) + shift


def _mod_kernel(c_ref, w_ref, b_ref, o_ref):
    c = c_ref[...]
    ca = (c * jax.nn.sigmoid(c)).astype(BF16)
    o_ref[...] = jnp.dot(ca, w_ref[...].astype(BF16), preferred_element_type=F32) + b_ref[...]


def _modulation(c, w_ada, b_ada):
    L, D, N = w_ada.shape
    B = c.shape[0]
    tn = 1536
    return pl.pallas_call(
        _mod_kernel,
        grid=(L, N // tn),
        in_specs=[pl.BlockSpec((B, D), lambda l, j: (0, 0)),
                  pl.BlockSpec((None, D, tn), lambda l, j: (l, 0, j)),
                  pl.BlockSpec((None, 1, tn), lambda l, j: (l, 0, j))],
        out_specs=pl.BlockSpec((None, B, tn), lambda l, j: (l, 0, j)),
        out_shape=jax.ShapeDtypeStruct((L, B, N), F32),
        compiler_params=_params(("parallel", "parallel")),
        name="adaln_mod",
    )(c, w_ada, b_ada.reshape(L, 1, N))


def _proj_kernel(x_ref, mod_ref, g_ref, w_ref, b_ref, wvt_ref, bvt_ref, cos_ref, sin_ref, wsp_ref, bsp_ref,
                 lng_ref, lnb_ref, q_ref, k_ref, vt_ref, hc_ref, tg_ref):
    tm = x_ref.shape[0]
    h = _ada_norm(x_ref[...], g_ref[...], mod_ref[1:2, :], mod_ref[0:1, :]).astype(BF16)

    vt = lax.dot_general(wvt_ref[...], h, (((1,), (1,)), ((), ())), preferred_element_type=F32)
    vt_ref[...] = (vt + bvt_ref[...]).astype(BF16)

    qkv = jnp.dot(h, w_ref[:, 0:2 * QK_COLS], preferred_element_type=F32) + b_ref[:, 0:2 * QK_COLS]
    cos = jnp.concatenate([cos_ref[...]] * (QK_COLS // LANES), axis=1)
    sin = jnp.concatenate([sin_ref[...]] * (QK_COLS // LANES), axis=1)
    lane = lax.broadcasted_iota(jnp.int32, (tm, QK_COLS), 1)
    first_half = (lane & (ATT_HEAD_DIM - 1)) < (ATT_HEAD_DIM // 2)

    def rope(t):
        fwd = pltpu.roll(t, QK_COLS - ATT_HEAD_DIM // 2, axis=1)
        bwd = pltpu.roll(t, ATT_HEAD_DIM // 2, axis=1)
        return t * cos + jnp.where(first_half, fwd, bwd) * sin

    q_ref[...] = (rope(qkv[:, 0:QK_COLS]) * (ATT_HEAD_DIM ** -0.5)).astype(BF16)
    k_ref[...] = rope(qkv[:, QK_COLS:2 * QK_COLS]).astype(BF16)

    c0 = QKV_COLS
    cv = jnp.dot(h, w_ref[:, c0:c0 + CONV_IN_COLS], preferred_element_type=F32) + b_ref[:, c0:c0 + CONV_IN_COLS]
    hc_ref[...] = (cv[:, 0:CONV_CH] * jax.nn.sigmoid(cv[:, CONV_CH:])).astype(BF16)

    c1 = c0 + CONV_IN_COLS
    z = jax.nn.gelu(jnp.dot(h, w_ref[:, c1:c1 + GMLP_IN_COLS], preferred_element_type=F32)
                    + b_ref[:, c1:c1 + GMLP_IN_COLS])
    u = z[:, 0:GMLP_CH]
    vv = z[:, GMLP_CH:]
    mu = jnp.mean(vv, axis=-1, keepdims=True)
    vc = vv - mu
    var = jnp.mean(vc * vc, axis=-1, keepdims=True)
    vn = (vc * lax.rsqrt(var + EPS) * lng_ref[...] + lnb_ref[...]).astype(BF16)
    row = lax.broadcasted_iota(jnp.int32, (GMLP_BLOCK, GMLP_BLOCK), 0)
    col = lax.broadcasted_iota(jnp.int32, (GMLP_BLOCK, GMLP_BLOCK), 1)
    gc = GMLP_CH // GMLP_GROUPS
    for g in range(GMLP_GROUPS):
        ws = jnp.where(row >= col, wsp_ref[g], 0.0).astype(BF16)
        bias = bsp_ref[g]
        for n in range(tm // GMLP_BLOCK):
            rs = slice(n * GMLP_BLOCK, (n + 1) * GMLP_BLOCK)
            cs = slice(g * gc, (g + 1) * gc)
            sg = jnp.dot(ws, vn[rs, cs], preferred_element_type=F32) + bias
            tg_ref[rs, cs] = (u[rs, cs] * sg).astype(BF16)


def _proj(x, mod, ln_g, w, b, wvt, bvt, cos_t, sin_t, w_sp, b_sp_t, gln_g, gln_b, tm):
    B, S, D = x.shape
    tok = lambda n: pl.BlockSpec((None, tm, n), lambda bi, i: (bi, i, 0))
    tok_out = jax.ShapeDtypeStruct((B, S, 512), BF16)
    vt_out = jax.ShapeDtypeStruct((B, S // tm, ATT_WIDTH, tm), BF16)
    vt_spec = pl.BlockSpec((None, None, ATT_WIDTH, tm), lambda bi, i: (bi, i, 0, 0))
    return pl.pallas_call(
        _proj_kernel,
        grid=(B, S // tm),
        in_specs=[tok(D),
                  pl.BlockSpec((None, 6, D), lambda bi, i: (bi, 0, 0)),
                  _const_spec((1, D)),
                  _const_spec((D, BRANCH_COLS)),
                  _const_spec((1, BRANCH_COLS)),
                  _const_spec((ATT_WIDTH, D)), _const_spec((ATT_WIDTH, 1)),
                  tok(LANES), tok(LANES),
                  _const_spec((GMLP_GROUPS, GMLP_BLOCK, GMLP_BLOCK)),
                  _const_spec((GMLP_GROUPS, GMLP_BLOCK, GMLP_BLOCK)),
                  _const_spec((1, GMLP_CH)), _const_spec((1, GMLP_CH))],
        out_specs=[tok(512), tok(512), vt_spec, tok(512), tok(512)],
        out_shape=[tok_out, tok_out, vt_out, tok_out, tok_out],
        compiler_params=_params(("parallel", "parallel")),
        name="in_proj",
    )(x, mod, ln_g, w, b, wvt, bvt, cos_t, sin_t, w_sp, b_sp_t, gln_g, gln_b)


def _col_max(s):
    r = s.shape[0]
    while r > SUBLANES and r % 2 == 0:
        r //= 2
        s = jnp.maximum(s[:r], s[r:])
    return jnp.max(s, axis=0, keepdims=True)


def _attn_kernel(lq1_ref, lk1_ref, lq2_ref, lk2_ref, q_ref, k_ref, vt_ref, g_ref, o_ref,
                 m_ref, acc_ref, sa_ref, sb_ref, mxa_ref, mxb_ref, *, t, lam_init):
    i = pl.program_id(2)
    q = q_ref[...]
    lane = lax.broadcasted_iota(jnp.int32, q.shape, 1)
    zero = jnp.zeros_like(q)
    qs = jnp.concatenate([jnp.where(lane < ATT_HEAD_DIM, q, zero),
                          jnp.where(lane >= ATT_HEAD_DIM, q, zero)], axis=0)
    m_ref[...] = jnp.full_like(m_ref, NEG_INF)
    acc_ref[...] = jnp.zeros_like(acc_ref)
    ones = jnp.ones((SUM_ROWS, t), BF16)

    def scores(j, s_ref, mx_ref):
        kj = k_ref[pl.ds(pl.multiple_of(j * t, t), t), :]
        s = lax.dot_general(kj, qs, (((1,), (1,)), ((), ())), preferred_element_type=F32)
        s_ref[...] = s
        mx_ref[...] = _col_max(s)

    def accumulate(j, s_ref, mx_ref, masked):
        s = s_ref[...]
        if masked:
            ck = lax.broadcasted_iota(jnp.int32, (t, t), 0)
            rq = lax.broadcasted_iota(jnp.int32, (t, t), 1)
            keep = (rq >> CHUNK_SHIFT) >= (ck >> CHUNK_SHIFT)
            s = jnp.where(jnp.concatenate([keep, keep], axis=1), s, NEG_INF)
            tile_max = _col_max(s)
        else:
            tile_max = mx_ref[...]
        m_old = m_ref[...]
        m_new = jnp.maximum(m_old, tile_max)
        p = jnp.exp(s - m_new).astype(BF16)
        v1 = jnp.concatenate([vt_ref[j], ones], axis=0)
        acc_ref[...] = jnp.exp(m_old - m_new) * acc_ref[...] + jnp.dot(v1, p, preferred_element_type=F32)
        m_ref[...] = m_new

    a, b = (sa_ref, mxa_ref), (sb_ref, mxb_ref)
    scores(0, *a)

    def body(jj, carry):
        j = 2 * jj
        scores(j + 1, *b)
        accumulate(j, *a, False)
        scores(j + 2, *a)
        accumulate(j + 1, *b, False)
        return carry

    lax.fori_loop(0, i // 2, body, 0)

    @pl.when(i % 2 == 1)
    def _():
        scores(i, *b)
        accumulate(i - 1, *a, False)
        accumulate(i, *b, True)

    @pl.when(i % 2 == 0)
    def _():
        accumulate(i, *a, True)

    lam = (jnp.exp(jnp.sum(lq1_ref[...] * lk1_ref[...], axis=-1, keepdims=True))
           - jnp.exp(jnp.sum(lq2_ref[...] * lk2_ref[...], axis=-1, keepdims=True)) + lam_init)
    o_all = acc_ref[0:ATT_V_DIM, :] / acc_ref[ATT_V_DIM:ATT_V_DIM + 1, :]
    o = (o_all[:, 0:t] - lam * o_all[:, t:2 * t]).T
    r = lax.rsqrt(jnp.mean(o * o, axis=-1, keepdims=True) + EPS)
    o_ref[...] = (o * r * g_ref[...] * (1.0 - lam_init)).astype(BF16)


def _attention(q, k, vt, lq1, lk1, lq2, lk2, subln_g, lam_init, t):
    B, S, _ = q.shape
    lam_spec = _const_spec((1, ATT_HEAD_DIM))
    k_spec = pl.BlockSpec((None, S, ATT_V_DIM), lambda b, h, i: (b, 0, h))
    vt_spec = pl.BlockSpec((None, S // t, ATT_V_DIM, t), lambda b, h, i: (b, 0, h, 0))
    qo_spec = pl.BlockSpec((None, t, ATT_V_DIM), lambda b, h, i: (b, i, h))
    return pl.pallas_call(
        functools.partial(_attn_kernel, t=t, lam_init=lam_init),
        grid=(B, ATT_HEADS, S // t),
        in_specs=[lam_spec, lam_spec, lam_spec, lam_spec, qo_spec, k_spec, vt_spec,
                  _const_spec((1, ATT_V_DIM))],
        out_specs=qo_spec,
        out_shape=jax.ShapeDtypeStruct((B, S, ATT_WIDTH), BF16),
        scratch_shapes=[pltpu.VMEM((1, 2 * t), F32), pltpu.VMEM((ATT_V_DIM + SUM_ROWS, 2 * t), F32),
                        pltpu.VMEM((t, 2 * t), F32), pltpu.VMEM((t, 2 * t), F32),
                        pltpu.VMEM((1, 2 * t), F32), pltpu.VMEM((1, 2 * t), F32)],
        compiler_params=_params(("parallel", "parallel", "parallel")),
        name="diff_attn",
    )(lq1, lk1, lq2, lk2, q, k, vt, subln_g)


def _merge_kernel(x_ref, mod_ref, g_ref, wg_ref, bg_ref, o_ref, wa_ref, hcp_ref, hc_ref, cw_ref, cb_ref,
                  clg_ref, clb_ref, wc_ref, tg_ref, wm_ref, wo_ref, y_ref, ext_ref, cn_ref):
    tm = x_ref.shape[0]
    i = pl.program_id(1)
    x = x_ref[...]
    h = _ada_norm(x, g_ref[...], mod_ref[1:2, :], mod_ref[0:1, :]).astype(BF16)
    gates = jax.nn.sigmoid(jnp.dot(h, wg_ref[...], preferred_element_type=F32) + bg_ref[...])

    y_att = jnp.dot(o_ref[...], wa_ref[...], preferred_element_type=F32)
    mixed = gates[:, 0:D_MODEL] * y_att

    prev = hcp_ref[...].astype(F32)
    ext_ref[0, 0:CONV_HALO, :] = jnp.where(i > 0, prev, jnp.zeros_like(prev))
    ext_ref[0, CONV_HALO:, :] = hc_ref[...].astype(F32)
    n_sh = tm + CONV_HALO - SUBLANES
    for s in range(1, SUBLANES):
        ext_ref[s, 0:n_sh, :] = ext_ref[0, s:s + n_sh, :]
    rb = 32
    first = CONV_HALO - (CONV_WIDTH - 1)

    def conv_rows(r, carry):
        base = pl.multiple_of(r * rb, rb)
        acc = jnp.zeros((rb, CONV_CH), F32) + cb_ref[...]
        for j in range(CONV_WIDTH):
            s = (first + j) % SUBLANES
            acc = acc + ext_ref[s, pl.ds(base + (first + j - s), rb), :] * cw_ref[j:j + 1, :]
        mu = jnp.mean(acc, axis=-1, keepdims=True)
        ac = acc - mu
        var = jnp.mean(ac * ac, axis=-1, keepdims=True)
        yn = ac * lax.rsqrt(var + EPS) * clg_ref[...] + clb_ref[...]
        cn_ref[pl.ds(base, rb), :] = (yn * jax.nn.sigmoid(yn)).astype(BF16)
        return carry

    lax.fori_loop(0, tm // rb, conv_rows, 0)
    y_conv = jnp.dot(cn_ref[...], wc_ref[...], preferred_element_type=F32)
    mixed = mixed + gates[:, D_MODEL:2 * D_MODEL] * y_conv

    y_gmlp = jnp.dot(tg_ref[...], wm_ref[...], preferred_element_type=F32)
    mixed = mixed + gates[:, 2 * D_MODEL:] * y_gmlp

    y_ref[...] = x + mod_ref[2:3, :] * jnp.dot(mixed.astype(BF16), wo_ref[...], preferred_element_type=F32)


def _merge(x, mod, ln_g, w_gate, b_gate, o, w_att, hc, cw, cb, clg, clb, w_conv, tg, w_gmlp, w_o, tm):
    B, S, D = x.shape
    tok = lambda n: pl.BlockSpec((None, tm, n), lambda bi, i: (bi, i, 0))
    halo = pl.BlockSpec((None, CONV_HALO, CONV_CH),
                        lambda bi, i: (bi, jnp.maximum(i * (tm // CONV_HALO) - 1, 0), 0))
    return pl.pallas_call(
        _merge_kernel,
        grid=(B, S // tm),
        in_specs=[tok(D),
                  pl.BlockSpec((None, 6, D), lambda bi, i: (bi, 0, 0)),
                  _const_spec((1, D)),
                  _const_spec((D, GATE_COLS)), _const_spec((1, GATE_COLS)),
                  tok(ATT_WIDTH), _const_spec((ATT_WIDTH, D)),
                  halo, tok(CONV_CH),
                  _const_spec((CONV_WIDTH, CONV_CH)), _const_spec((1, CONV_CH)),
                  _const_spec((1, CONV_CH)), _const_spec((1, CONV_CH)),
                  _const_spec((CONV_CH, D)),
                  tok(GMLP_CH), _const_spec((GMLP_CH, D)),
                  _const_spec((D, D))],
        out_specs=tok(D),
        out_shape=jax.ShapeDtypeStruct((B, S, D), F32),
        scratch_shapes=[pltpu.VMEM((SUBLANES, CONV_HALO + tm, CONV_CH), F32), pltpu.VMEM((tm, CONV_CH), BF16)],
        compiler_params=_params(("parallel", "parallel")),
        name="merge",
    )(x, mod, ln_g, w_gate, b_gate, o, w_att, hc, hc, cw, cb, clg, clb, w_conv, tg, w_gmlp, w_o)


def _ffn_kernel(x_ref, mod_ref, g_ref, wu_ref, cw_ref, cb_ref, wd_ref, fg_ref, y_ref,
                h_ref, ext_ref, carry_ref, acc_ref, *, final):
    tm = x_ref.shape[0]
    nc, _, fc2 = wu_ref.shape
    fc = fc2 // 2
    i = pl.program_id(1)
    h_ref[...] = _ada_norm(x_ref[...], g_ref[...], mod_ref[4:5, :], mod_ref[3:4, :]).astype(BF16)
    acc_ref[...] = jnp.zeros_like(acc_ref)

    @pl.when(i == 0)
    def _():
        carry_ref[...] = jnp.zeros_like(carry_ref)

    def chunk(c, carry):
        up = jnp.dot(h_ref[...], wu_ref[c], preferred_element_type=F32)
        ext_ref[0:SUBLANES, :] = carry_ref[c]
        ext_ref[SUBLANES:, :] = up
        carry_ref[c] = up[tm - SUBLANES:, :]
        cw = cw_ref[c]
        out = cb_ref[c] + up * cw[2:3, :]
        out = out + ext_ref[SUBLANES - 1:SUBLANES - 1 + tm, :] * cw[1:2, :]
        out = out + ext_ref[SUBLANES - 2:SUBLANES - 2 + tm, :] * cw[0:1, :]
        gate = out[:, 0:fc]
        act = (gate * jax.nn.sigmoid(gate) * out[:, fc:]).astype(BF16)
        acc_ref[...] += jnp.dot(act, wd_ref[c], preferred_element_type=F32)
        return carry

    lax.fori_loop(0, nc, chunk, 0)
    y = x_ref[...] + mod_ref[5:6, :] * acc_ref[...]
    if final:
        r = lax.rsqrt(jnp.mean(y * y, axis=-1, keepdims=True) + EPS)
        y = y * r * fg_ref[...]
    y_ref[...] = y


def _ffn(x, mod, ln_g, w_up_c, cw_c, cb_c, w_down_c, final_g, tm, final):
    B, S, D = x.shape
    nc, _, fc2 = w_up_c.shape
    tok = pl.BlockSpec((None, tm, D), lambda bi, i: (bi, i, 0))
    return pl.pallas_call(
        functools.partial(_ffn_kernel, final=final),
        grid=(B, S // tm),
        in_specs=[tok,
                  pl.BlockSpec((None, 6, D), lambda bi, i: (bi, 0, 0)),
                  _const_spec((1, D)),
                  _const_spec((nc, D, fc2)),
                  _const_spec((nc, FFN_CONV_WIDTH, fc2)), _const_spec((nc, 1, fc2)),
                  _const_spec((nc, fc2 // 2, D)),
                  _const_spec((1, D))],
        out_specs=tok,
        out_shape=jax.ShapeDtypeStruct((B, S, D), F32),
        scratch_shapes=[pltpu.VMEM((tm, D), BF16), pltpu.VMEM((SUBLANES + tm, fc2), F32),
                        pltpu.VMEM((nc, SUBLANES, fc2), F32), pltpu.VMEM((tm, D), F32)],
        compiler_params=_params(("parallel", "arbitrary")),
        name="ffn",
    )(x, mod, ln_g, w_up_c, cw_c, cb_c, w_down_c, final_g)


def _chunk_cols(a, fc):
    lead = a.shape[:-1]
    g = a[..., :FFN_DIM].reshape(*lead, FFN_DIM // fc, fc)
    v = a[..., FFN_DIM:].reshape(*lead, FFN_DIM // fc, fc)
    return jnp.moveaxis(jnp.concatenate([g, v], axis=-1), -2, 0)


def _tile(s, pref):
    return pref if s % pref == 0 else s


def kernel(x, c, positions, ln1_g, ln2_g, w_ada, b_ada, w_in, b_in, lambda_q1, lambda_k1, lambda_q2, lambda_k2,
           attn_subln_g, w_attn_out, conv_dw_w, conv_dw_b, conv_ln_g, conv_ln_b, w_conv_out, gmlp_ln_g, gmlp_ln_b,
           w_spatial, b_spatial, w_gmlp_out, w_o, w_up, ffn_dw_w, ffn_dw_b, w_down, final_g):
    B, S, D = x.shape
    L = w_in.shape[0]
    tm = _tile(S, TOKEN_TILE)
    ta = tm

    inv_freq = 1.0 / (ROPE_THETA ** (jnp.arange(0, ATT_HEAD_DIM, 2, dtype=F32) / ATT_HEAD_DIM))
    ang = positions.astype(F32)[..., None] * inv_freq
    cos, sin = jnp.cos(ang), jnp.sin(ang)
    cos_t = jnp.concatenate([cos, cos, cos, cos], axis=-1)
    sin_t = jnp.concatenate([-sin, sin, -sin, sin], axis=-1)

    mod = _modulation(c, w_ada, b_ada).reshape(L, B, 6, D)
    row = lambda a: a.reshape(1, -1)

    for l in range(L):
        lam_init = 0.8 - 0.6 * math.exp(-0.3 * l)
        w_l = w_in[l].astype(BF16)
        q, k, vt, hc, tg = _proj(x, mod[l], row(ln1_g[l]), w_l[:, :BRANCH_COLS], row(b_in[l, :BRANCH_COLS]),
                                w_l[:, 2 * QK_COLS:QKV_COLS].T, b_in[l, 2 * QK_COLS:QKV_COLS].reshape(-1, 1),
                                cos_t, sin_t, w_spatial[l],
                                jnp.broadcast_to(b_spatial[l][:, :, None], w_spatial[l].shape),
                                row(gmlp_ln_g[l]), row(gmlp_ln_b[l]), tm)
        o = _attention(q, k, vt, row(lambda_q1[l]), row(lambda_k1[l]), row(lambda_q2[l]), row(lambda_k2[l]),
                       row(attn_subln_g[l]), lam_init, ta)
        x = _merge(x, mod[l], row(ln1_g[l]), w_l[:, BRANCH_COLS:], row(b_in[l, BRANCH_COLS:]),
                   o, w_attn_out[l].astype(BF16), hc, conv_dw_w[l], row(conv_dw_b[l]),
                   row(conv_ln_g[l]), row(conv_ln_b[l]), w_conv_out[l].astype(BF16),
                   tg, w_gmlp_out[l].astype(BF16), w_o[l].astype(BF16), tm)
        x = _ffn(x, mod[l], row(ln2_g[l]), _chunk_cols(w_up[l].astype(BF16), FFN_CHUNK),
                 _chunk_cols(ffn_dw_w[l], FFN_CHUNK), _chunk_cols(row(ffn_dw_b[l]), FFN_CHUNK),
                 w_down[l].astype(BF16).reshape(FFN_DIM // FFN_CHUNK, FFN_CHUNK, D), row(final_g),
                 tm, l == L - 1)
    return x
```

```python
import functools
import math

import jax
import jax.numpy as jnp
from jax import lax
from jax.experimental import pallas as pl
from jax.experimental.pallas import tpu as pltpu

D_MODEL = 1024
CHUNK = 64
ATT_HEADS = 4
ATT_HEAD_DIM = 64
ATT_V_DIM = 128
ATT_WIDTH = 512
QK_COLS = 512
ROPE_THETA = 10000.0
CONV_CH = 512
CONV_WIDTH = 31
GMLP_CH = 512
GMLP_GROUPS = 4
GMLP_BLOCK = 128
FFN_DIM = 2816
FFN_CONV_WIDTH = 3
EPS = 1e-6
QKV_COLS = 2 * QK_COLS + ATT_WIDTH
CONV_IN_COLS = 2 * CONV_CH
GMLP_IN_COLS = 2 * GMLP_CH
BRANCH_COLS = QKV_COLS + CONV_IN_COLS + GMLP_IN_COLS
GATE_COLS = 3 * D_MODEL

LANES = 128
SUBLANES = 8
CONV_HALO = 32
VMEM_LIMIT = 56 * 1024 * 1024
NEG_INF = -1e30
CHUNK_SHIFT = CHUNK.bit_length() - 1
TOKEN_TILE = 512
SUM_ROWS = 16
FFN_CHUNK = 256

F32 = jnp.float32
BF16 = jnp.bfloat16


def _const_spec(shape):
    n = len(shape)
    return pl.BlockSpec(shape, lambda *_: (0,) * n, pipeline_mode=pl.Buffered(1))


def _params(sem):
    return pltpu.CompilerParams(dimension_semantics=sem, vmem_limit_bytes=VMEM_LIMIT)


def _sigmoid(x):
    return 0.5 * jnp.tanh(0.5 * x) + 0.5


def _ada_norm(x, g, scale, shift):
    r = lax.rsqrt(jnp.mean(x * x, axis=-1, keepdims=True) + EPS)
    return x * r * (g * (1.0 + scale)) + shift


def _mod_kernel(c_ref, w_ref, b_ref, o_ref):
    c = c_ref[...]
    ca = (c * _sigmoid(c)).astype(BF16)
    o_ref[...] = jnp.dot(ca, w_ref[...].astype(BF16), preferred_element_type=F32) + b_ref[...]


def _modulation(c, w_ada, b_ada):
    L, D, N = w_ada.shape
    B = c.shape[0]
    tn = 1536
    return pl.pallas_call(
        _mod_kernel,
        grid=(L, N // tn),
        in_specs=[pl.BlockSpec((B, D), lambda l, j: (0, 0)),
                  pl.BlockSpec((None, D, tn), lambda l, j: (l, 0, j)),
                  pl.BlockSpec((None, 1, tn), lambda l, j: (l, 0, j))],
        out_specs=pl.BlockSpec((None, B, tn), lambda l, j: (l, 0, j)),
        out_shape=jax.ShapeDtypeStruct((L, B, N), F32),
        compiler_params=_params(("parallel", "parallel")),
        name="adaln_mod",
    )(c, w_ada, b_ada.reshape(L, 1, N))


def _proj_kernel(x_ref, mod_ref, g_ref, w_ref, b_ref, wvt_ref, bvt_ref, cos_ref, sin_ref, wsp_ref, bsp_ref,
                 lng_ref, lnb_ref, q_ref, k_ref, vt_ref, hc_ref, tg_ref):
    tm = x_ref.shape[0]
    h = _ada_norm(x_ref[...], g_ref[...], mod_ref[1:2, :], mod_ref[0:1, :]).astype(BF16)

    vt = lax.dot_general(wvt_ref[...], h, (((1,), (1,)), ((), ())), preferred_element_type=F32)
    vt_ref[...] = (vt + bvt_ref[...]).astype(BF16)

    qkv = jnp.dot(h, w_ref[:, 0:2 * QK_COLS], preferred_element_type=F32) + b_ref[:, 0:2 * QK_COLS]
    cos = jnp.concatenate([cos_ref[...]] * (QK_COLS // LANES), axis=1)
    sin = jnp.concatenate([sin_ref[...]] * (QK_COLS // LANES), axis=1)
    lane = lax.broadcasted_iota(jnp.int32, (tm, QK_COLS), 1)
    first_half = (lane & (ATT_HEAD_DIM - 1)) < (ATT_HEAD_DIM // 2)

    def rope(t):
        fwd = pltpu.roll(t, QK_COLS - ATT_HEAD_DIM // 2, axis=1)
        bwd = pltpu.roll(t, ATT_HEAD_DIM // 2, axis=1)
        return t * cos + jnp.where(first_half, fwd, bwd) * sin

    q_ref[...] = (rope(qkv[:, 0:QK_COLS]) * (ATT_HEAD_DIM ** -0.5)).astype(BF16)
    k_ref[...] = rope(qkv[:, QK_COLS:2 * QK_COLS]).astype(BF16)

    c0 = QKV_COLS
    cv = jnp.dot(h, w_ref[:, c0:c0 + CONV_IN_COLS], preferred_element_type=F32) + b_ref[:, c0:c0 + CONV_IN_COLS]
    hc_ref[...] = (cv[:, 0:CONV_CH] * _sigmoid(cv[:, CONV_CH:])).astype(BF16)

    c1 = c0 + CONV_IN_COLS
    z = jax.nn.gelu(jnp.dot(h, w_ref[:, c1:c1 + GMLP_IN_COLS], preferred_element_type=F32)
                    + b_ref[:, c1:c1 + GMLP_IN_COLS])
    u = z[:, 0:GMLP_CH]
    vv = z[:, GMLP_CH:]
    mu = jnp.mean(vv, axis=-1, keepdims=True)
    vc = vv - mu
    var = jnp.mean(vc * vc, axis=-1, keepdims=True)
    vn = (vc * lax.rsqrt(var + EPS) * lng_ref[...] + lnb_ref[...]).astype(BF16)
    row = lax.broadcasted_iota(jnp.int32, (GMLP_BLOCK, GMLP_BLOCK), 0)
    col = lax.broadcasted_iota(jnp.int32, (GMLP_BLOCK, GMLP_BLOCK), 1)
    gc = GMLP_CH // GMLP_GROUPS
    for g in range(GMLP_GROUPS):
        ws = jnp.where(row >= col, wsp_ref[g], 0.0).astype(BF16)
        bias = bsp_ref[g]
        for n in range(tm // GMLP_BLOCK):
            rs = slice(n * GMLP_BLOCK, (n + 1) * GMLP_BLOCK)
            cs = slice(g * gc, (g + 1) * gc)
            sg = jnp.dot(ws, vn[rs, cs], preferred_element_type=F32) + bias
            tg_ref[rs, cs] = (u[rs, cs] * sg).astype(BF16)


def _proj(x, mod, ln_g, w, b, wvt, bvt, cos_t, sin_t, w_sp, b_sp_t, gln_g, gln_b, tm):
    B, S, D = x.shape
    tok = lambda n: pl.BlockSpec((None, tm, n), lambda bi, i: (bi, i, 0))
    tok_out = jax.ShapeDtypeStruct((B, S, 512), BF16)
    vt_out = jax.ShapeDtypeStruct((B, S // tm, ATT_WIDTH, tm), BF16)
    vt_spec = pl.BlockSpec((None, None, ATT_WIDTH, tm), lambda bi, i: (bi, i, 0, 0))
    return pl.pallas_call(
        _proj_kernel,
        grid=(B, S // tm),
        in_specs=[tok(D),
                  pl.BlockSpec((None, 6, D), lambda bi, i: (bi, 0, 0)),
                  _const_spec((1, D)),
                  _const_spec((D, BRANCH_COLS)),
                  _const_spec((1, BRANCH_COLS)),
                  _const_spec((ATT_WIDTH, D)), _const_spec((ATT_WIDTH, 1)),
                  tok(LANES), tok(LANES),
                  _const_spec((GMLP_GROUPS, GMLP_BLOCK, GMLP_BLOCK)),
                  _const_spec((GMLP_GROUPS, GMLP_BLOCK, GMLP_BLOCK)),
                  _const_spec((1, GMLP_CH)), _const_spec((1, GMLP_CH))],
        out_specs=[tok(512), tok(512), vt_spec, tok(512), tok(512)],
        out_shape=[tok_out, tok_out, vt_out, tok_out, tok_out],
        compiler_params=_params(("parallel", "parallel")),
        name="in_proj",
    )(x, mod, ln_g, w, b, wvt, bvt, cos_t, sin_t, w_sp, b_sp_t, gln_g, gln_b)


def _col_max(s):
    r = s.shape[0]
    while r > SUBLANES and r % 2 == 0:
        r //= 2
        s = jnp.maximum(s[:r], s[r:])
    return jnp.max(s, axis=0, keepdims=True)


def _attn_kernel(lq1_ref, lk1_ref, lq2_ref, lk2_ref, q_ref, k_ref, vt_ref, g_ref, o_ref,
                 m_ref, acc_ref, sa_ref, sb_ref, mxa_ref, mxb_ref, *, t, lam_init):
    i = pl.program_id(2)
    q = q_ref[...]
    lane = lax.broadcasted_iota(jnp.int32, q.shape, 1)
    zero = jnp.zeros_like(q)
    qs = jnp.concatenate([jnp.where(lane < ATT_HEAD_DIM, q, zero),
                          jnp.where(lane >= ATT_HEAD_DIM, q, zero)], axis=0)
    m_ref[...] = jnp.full_like(m_ref, NEG_INF)
    acc_ref[...] = jnp.zeros_like(acc_ref)
    ones = jnp.ones((SUM_ROWS, t), BF16)

    def scores(j, s_ref, mx_ref):
        kj = k_ref[pl.ds(pl.multiple_of(j * t, t), t), :]
        s = lax.dot_general(kj, qs, (((1,), (1,)), ((), ())), preferred_element_type=F32)
        s_ref[...] = s
        mx_ref[...] = _col_max(s)

    def accumulate(j, s_ref, mx_ref, masked):
        s = s_ref[...]
        if masked:
            ck = lax.broadcasted_iota(jnp.int32, (t, t), 0)
            rq = lax.broadcasted_iota(jnp.int32, (t, t), 1)
            keep = (rq >> CHUNK_SHIFT) >= (ck >> CHUNK_SHIFT)
            s = jnp.where(jnp.concatenate([keep, keep], axis=1), s, NEG_INF)
            tile_max = _col_max(s)
        else:
            tile_max = mx_ref[...]
        m_old = m_ref[...]
        m_new = jnp.maximum(m_old, tile_max)
        p = jnp.exp(s - m_new).astype(BF16)
        v1 = jnp.concatenate([vt_ref[j], ones], axis=0)
        acc_ref[...] = jnp.exp(m_old - m_new) * acc_ref[...] + jnp.dot(v1, p, preferred_element_type=F32)
        m_ref[...] = m_new

    a, b = (sa_ref, mxa_ref), (sb_ref, mxb_ref)
    scores(0, *a)

    def body(jj, carry):
        j = 2 * jj
        scores(j + 1, *b)
        accumulate(j, *a, False)
        scores(j + 2, *a)
        accumulate(j + 1, *b, False)
        return carry

    lax.fori_loop(0, i // 2, body, 0)

    @pl.when(i % 2 == 1)
    def _():
        scores(i, *b)
        accumulate(i - 1, *a, False)
        accumulate(i, *b, True)

    @pl.when(i % 2 == 0)
    def _():
        accumulate(i, *a, True)

    lam = (jnp.exp(jnp.sum(lq1_ref[...] * lk1_ref[...], axis=-1, keepdims=True))
           - jnp.exp(jnp.sum(lq2_ref[...] * lk2_ref[...], axis=-1, keepdims=True)) + lam_init)
    o_all = acc_ref[0:ATT_V_DIM, :] / acc_ref[ATT_V_DIM:ATT_V_DIM + 1, :]
    o = (o_all[:, 0:t] - lam * o_all[:, t:2 * t]).T
    r = lax.rsqrt(jnp.mean(o * o, axis=-1, keepdims=True) + EPS)
    o_ref[...] = (o * r * g_ref[...] * (1.0 - lam_init)).astype(BF16)


def _attention(q, k, vt, lq1, lk1, lq2, lk2, subln_g, lam_init, t):
    B, S, _ = q.shape
    lam_spec = _const_spec((1, ATT_HEAD_DIM))
    k_spec = pl.BlockSpec((None, S, ATT_V_DIM), lambda b, h, i: (b, 0, h))
    vt_spec = pl.BlockSpec((None, S // t, ATT_V_DIM, t), lambda b, h, i: (b, 0, h, 0))
    qo_spec = pl.BlockSpec((None, t, ATT_V_DIM), lambda b, h, i: (b, i, h))
    return pl.pallas_call(
        functools.partial(_attn_kernel, t=t, lam_init=lam_init),
        grid=(B, ATT_HEADS, S // t),
        in_specs=[lam_spec, lam_spec, lam_spec, lam_spec, qo_spec, k_spec, vt_spec,
                  _const_spec((1, ATT_V_DIM))],
        out_specs=qo_spec,
        out_shape=jax.ShapeDtypeStruct((B, S, ATT_WIDTH), BF16),
        scratch_shapes=[pltpu.VMEM((1, 2 * t), F32), pltpu.VMEM((ATT_V_DIM + SUM_ROWS, 2 * t), F32),
                        pltpu.VMEM((t, 2 * t), F32), pltpu.VMEM((t, 2 * t), F32),
                        pltpu.VMEM((1, 2 * t), F32), pltpu.VMEM((1, 2 * t), F32)],
        compiler_params=_params(("parallel", "parallel", "parallel")),
        name="diff_attn",
    )(lq1, lk1, lq2, lk2, q, k, vt, subln_g)


def _merge_kernel(x_ref, mod_ref, g_ref, wg_ref, bg_ref, o_ref, wa_ref, hcp_ref, hc_ref, cw_ref, cb_ref,
                  clg_ref, clb_ref, wc_ref, tg_ref, wm_ref, wo_ref, y_ref, ext_ref, cn_ref):
    tm = x_ref.shape[0]
    i = pl.program_id(1)
    x = x_ref[...]
    h = _ada_norm(x, g_ref[...], mod_ref[1:2, :], mod_ref[0:1, :]).astype(BF16)

    def gated(branch, y):
        cols = slice(branch * D_MODEL, (branch + 1) * D_MODEL)
        gate = _sigmoid(jnp.dot(h, wg_ref[:, cols], preferred_element_type=F32) + bg_ref[:, cols])
        return gate * y

    mixed = gated(0, jnp.dot(o_ref[...], wa_ref[...], preferred_element_type=F32))
    mixed = mixed + gated(2, jnp.dot(tg_ref[...], wm_ref[...], preferred_element_type=F32))

    prev = hcp_ref[...].astype(F32)
    ext_ref[0, 0:CONV_HALO, :] = jnp.where(i > 0, prev, jnp.zeros_like(prev))
    ext_ref[0, CONV_HALO:, :] = hc_ref[...].astype(F32)
    n_sh = tm + CONV_HALO - SUBLANES
    for s in range(1, SUBLANES):
        ext_ref[s, 0:n_sh, :] = ext_ref[0, s:s + n_sh, :]
    rb = 32
    first = CONV_HALO - (CONV_WIDTH - 1)

    for base in range(0, tm, rb):
        acc = jnp.zeros((rb, CONV_CH), F32) + cb_ref[...]
        for j in range(CONV_WIDTH):
            s = (first + j) % SUBLANES
            r0 = base + (first + j - s)
            acc = acc + ext_ref[s, r0:r0 + rb, :] * jnp.tile(cw_ref[j], (rb // SUBLANES, 1))
        mu = jnp.mean(acc, axis=-1, keepdims=True)
        ac = acc - mu
        var = jnp.mean(ac * ac, axis=-1, keepdims=True)
        yn = ac * lax.rsqrt(var + EPS) * clg_ref[...] + clb_ref[...]
        cn_ref[base:base + rb, :] = (yn * _sigmoid(yn)).astype(BF16)

    mixed = mixed + gated(1, jnp.dot(cn_ref[...], wc_ref[...], preferred_element_type=F32))
    y_ref[...] = x + mod_ref[2:3, :] * jnp.dot(mixed.astype(BF16), wo_ref[...], preferred_element_type=F32)


def _merge(x, mod, ln_g, w_gate, b_gate, o, w_att, hc, cw, cb, clg, clb, w_conv, tg, w_gmlp, w_o, tm):
    B, S, D = x.shape
    tok = lambda n: pl.BlockSpec((None, tm, n), lambda bi, i: (bi, i, 0))
    halo = pl.BlockSpec((None, CONV_HALO, CONV_CH),
                        lambda bi, i: (bi, jnp.maximum(i * (tm // CONV_HALO) - 1, 0), 0))
    return pl.pallas_call(
        _merge_kernel,
        grid=(B, S // tm),
        in_specs=[tok(D),
                  pl.BlockSpec((None, 6, D), lambda bi, i: (bi, 0, 0)),
                  _const_spec((1, D)),
                  _const_spec((D, GATE_COLS)), _const_spec((1, GATE_COLS)),
                  tok(ATT_WIDTH), _const_spec((ATT_WIDTH, D)),
                  halo, tok(CONV_CH),
                  _const_spec((CONV_WIDTH, SUBLANES, CONV_CH)), _const_spec((1, CONV_CH)),
                  _const_spec((1, CONV_CH)), _const_spec((1, CONV_CH)),
                  _const_spec((CONV_CH, D)),
                  tok(GMLP_CH), _const_spec((GMLP_CH, D)),
                  _const_spec((D, D))],
        out_specs=tok(D),
        out_shape=jax.ShapeDtypeStruct((B, S, D), F32),
        scratch_shapes=[pltpu.VMEM((SUBLANES, CONV_HALO + tm, CONV_CH), F32), pltpu.VMEM((tm, CONV_CH), BF16)],
        compiler_params=_params(("parallel", "parallel")),
        name="merge",
    )(x, mod, ln_g, w_gate, b_gate, o, w_att, hc, hc, cw, cb, clg, clb, w_conv, tg, w_gmlp, w_o)


def _ffn_kernel(x_ref, mod_ref, g_ref, wu_ref, cw_ref, cb_ref, wd_ref, fg_ref, y_ref,
                h_ref, exta_ref, extb_ref, carry_ref, acc_ref, *, final):
    tm = x_ref.shape[0]
    nc, _, fc2 = wu_ref.shape
    fc = fc2 // 2
    i = pl.program_id(1)
    h_ref[...] = _ada_norm(x_ref[...], g_ref[...], mod_ref[4:5, :], mod_ref[3:4, :]).astype(BF16)
    acc_ref[...] = jnp.zeros_like(acc_ref)

    @pl.when(i == 0)
    def _():
        carry_ref[...] = jnp.zeros_like(carry_ref)

    def up_proj(c, ext_ref):
        up = jnp.dot(h_ref[...], wu_ref[c], preferred_element_type=F32)
        ext_ref[0:SUBLANES, :] = carry_ref[c]
        ext_ref[SUBLANES:, :] = up
        carry_ref[c] = up[tm - SUBLANES:, :]

    def down_proj(c, ext_ref):
        cw = cw_ref[c]
        out = cb_ref[c] + ext_ref[SUBLANES:, :] * cw[2:3, :]
        out = out + ext_ref[SUBLANES - 1:SUBLANES - 1 + tm, :] * cw[1:2, :]
        out = out + ext_ref[SUBLANES - 2:SUBLANES - 2 + tm, :] * cw[0:1, :]
        gate = out[:, 0:fc]
        act = (gate * _sigmoid(gate) * out[:, fc:]).astype(BF16)
        acc_ref[...] += jnp.dot(act, wd_ref[c], preferred_element_type=F32)

    assert nc % 2 == 1
    up_proj(0, exta_ref)

    def pair(jj, carry):
        c = 2 * jj
        up_proj(c + 1, extb_ref)
        down_proj(c, exta_ref)
        up_proj(c + 2, exta_ref)
        down_proj(c + 1, extb_ref)
        return carry

    lax.fori_loop(0, nc // 2, pair, 0)
    down_proj(nc - 1, exta_ref)
    y = x_ref[...] + mod_ref[5:6, :] * acc_ref[...]
    if final:
        r = lax.rsqrt(jnp.mean(y * y, axis=-1, keepdims=True) + EPS)
        y = y * r * fg_ref[...]
    y_ref[...] = y


def _ffn(x, mod, ln_g, w_up_c, cw_c, cb_c, w_down_c, final_g, tm, final):
    B, S, D = x.shape
    nc, _, fc2 = w_up_c.shape
    tok = pl.BlockSpec((None, tm, D), lambda bi, i: (bi, i, 0))
    return pl.pallas_call(
        functools.partial(_ffn_kernel, final=final),
        grid=(B, S // tm),
        in_specs=[tok,
                  pl.BlockSpec((None, 6, D), lambda bi, i: (bi, 0, 0)),
                  _const_spec((1, D)),
                  _const_spec((nc, D, fc2)),
                  _const_spec((nc, FFN_CONV_WIDTH, fc2)), _const_spec((nc, 1, fc2)),
                  _const_spec((nc, fc2 // 2, D)),
                  _const_spec((1, D))],
        out_specs=tok,
        out_shape=jax.ShapeDtypeStruct((B, S, D), F32),
        scratch_shapes=[pltpu.VMEM((tm, D), BF16),
                        pltpu.VMEM((SUBLANES + tm, fc2), F32), pltpu.VMEM((SUBLANES + tm, fc2), F32),
                        pltpu.VMEM((nc, SUBLANES, fc2), F32), pltpu.VMEM((tm, D), F32)],
        compiler_params=_params(("parallel", "arbitrary")),
        name="ffn",
    )(x, mod, ln_g, w_up_c, cw_c, cb_c, w_down_c, final_g)


def _chunk_cols(a, fc):
    lead = a.shape[:-1]
    g = a[..., :FFN_DIM].reshape(*lead, FFN_DIM // fc, fc)
    v = a[..., FFN_DIM:].reshape(*lead, FFN_DIM // fc, fc)
    return jnp.moveaxis(jnp.concatenate([g, v], axis=-1), -2, 0)


def _tile(s, pref):
    return pref if s % pref == 0 else s


def kernel(x, c, positions, ln1_g, ln2_g, w_ada, b_ada, w_in, b_in, lambda_q1, lambda_k1, lambda_q2, lambda_k2,
           attn_subln_g, w_attn_out, conv_dw_w, conv_dw_b, conv_ln_g, conv_ln_b, w_conv_out, gmlp_ln_g, gmlp_ln_b,
           w_spatial, b_spatial, w_gmlp_out, w_o, w_up, ffn_dw_w, ffn_dw_b, w_down, final_g):
    B, S, D = x.shape
    L = w_in.shape[0]
    tm = _tile(S, TOKEN_TILE)
    ta = tm

    inv_freq = 1.0 / (ROPE_THETA ** (jnp.arange(0, ATT_HEAD_DIM, 2, dtype=F32) / ATT_HEAD_DIM))
    ang = positions.astype(F32)[..., None] * inv_freq
    cos, sin = jnp.cos(ang), jnp.sin(ang)
    cos_t = jnp.concatenate([cos, cos, cos, cos], axis=-1)
    sin_t = jnp.concatenate([-sin, sin, -sin, sin], axis=-1)

    mod = _modulation(c, w_ada, b_ada).reshape(L, B, 6, D)
    row = lambda a: a.reshape(1, -1)

    for l in range(L):
        lam_init = 0.8 - 0.6 * math.exp(-0.3 * l)
        w_l = w_in[l].astype(BF16)
        q, k, vt, hc, tg = _proj(x, mod[l], row(ln1_g[l]), w_l[:, :BRANCH_COLS], row(b_in[l, :BRANCH_COLS]),
                                w_l[:, 2 * QK_COLS:QKV_COLS].T, b_in[l, 2 * QK_COLS:QKV_COLS].reshape(-1, 1),
                                cos_t, sin_t, w_spatial[l],
                                jnp.broadcast_to(b_spatial[l][:, :, None], w_spatial[l].shape),
                                row(gmlp_ln_g[l]), row(gmlp_ln_b[l]), tm)
        o = _attention(q, k, vt, row(lambda_q1[l]), row(lambda_k1[l]), row(lambda_q2[l]), row(lambda_k2[l]),
                       row(attn_subln_g[l]), lam_init, ta)
        x = _merge(x, mod[l], row(ln1_g[l]), w_l[:, BRANCH_COLS:], row(b_in[l, BRANCH_COLS:]),
                   o, w_attn_out[l].astype(BF16), hc,
                   jnp.broadcast_to(conv_dw_w[l][:, None, :], (CONV_WIDTH, SUBLANES, CONV_CH)), row(conv_dw_b[l]),
                   row(conv_ln_g[l]), row(conv_ln_b[l]), w_conv_out[l].astype(BF16),
                   tg, w_gmlp_out[l].astype(BF16), w_o[l].astype(BF16), tm)
        x = _ffn(x, mod[l], row(ln2_g[l]), _chunk_cols(w_up[l].astype(BF16), FFN_CHUNK),
                 _chunk_cols(ffn_dw_w[l], FFN_CHUNK), _chunk_cols(row(ffn_dw_b[l]), FFN_CHUNK),
                 w_down[l].astype(BF16).reshape(FFN_DIM // FFN_CHUNK, FFN_CHUNK, D), row(final_g),
                 tm, l == L - 1)
    return x
```

```python
import functools
import math

import jax
import jax.numpy as jnp
from jax import lax
from jax.experimental import pallas as pl
from jax.experimental.pallas import tpu as pltpu

D_MODEL = 1024
CHUNK = 64
ATT_HEADS = 4
ATT_HEAD_DIM = 64
ATT_V_DIM = 128
ATT_WIDTH = 512
QK_COLS = 512
ROPE_THETA = 10000.0
CONV_CH = 512
CONV_WIDTH = 31
GMLP_CH = 512
GMLP_GROUPS = 4
GMLP_BLOCK = 128
FFN_DIM = 2816
FFN_CONV_WIDTH = 3
EPS = 1e-6
QKV_COLS = 2 * QK_COLS + ATT_WIDTH
CONV_IN_COLS = 2 * CONV_CH
GMLP_IN_COLS = 2 * GMLP_CH
BRANCH_COLS = QKV_COLS + CONV_IN_COLS + GMLP_IN_COLS
GATE_COLS = 3 * D_MODEL

LANES = 128
SUBLANES = 8
CONV_HALO = 32
VMEM_LIMIT = 56 * 1024 * 1024
NEG_INF = -1e30
CHUNK_SHIFT = CHUNK.bit_length() - 1
TOKEN_TILE = 512
SUM_ROWS = 16
FFN_CHUNK = 256

F32 = jnp.float32
BF16 = jnp.bfloat16


def _const_spec(shape):
    n = len(shape)
    return pl.BlockSpec(shape, lambda *_: (0,) * n, pipeline_mode=pl.Buffered(1))


def _params(sem):
    return pltpu.CompilerParams(dimension_semantics=sem, vmem_limit_bytes=VMEM_LIMIT)


def _sigmoid(x):
    return 0.5 * jnp.tanh(0.5 * x) + 0.5


def _ada_norm(x, g, scale, shift):
    r = lax.rsqrt(jnp.mean(x * x, axis=-1, keepdims=True) + EPS)
    return x * r * (g * (1.0 + scale)) + shift


def _mod_kernel(c_ref, w_ref, b_ref, o_ref):
    c = c_ref[...]
    ca = (c * _sigmoid(c)).astype(BF16)
    o_ref[...] = jnp.dot(ca, w_ref[...].astype(BF16), preferred_element_type=F32) + b_ref[...]


def _modulation(c, w_ada, b_ada):
    L, D, N = w_ada.shape
    B = c.shape[0]
    tn = 1536
    return pl.pallas_call(
        _mod_kernel,
        grid=(L, N // tn),
        in_specs=[pl.BlockSpec((B, D), lambda l, j: (0, 0)),
                  pl.BlockSpec((None, D, tn), lambda l, j: (l, 0, j)),
                  pl.BlockSpec((None, 1, tn), lambda l, j: (l, 0, j))],
        out_specs=pl.BlockSpec((None, B, tn), lambda l, j: (l, 0, j)),
        out_shape=jax.ShapeDtypeStruct((L, B, N), F32),
        compiler_params=_params(("parallel", "parallel")),
        name="adaln_mod",
    )(c, w_ada, b_ada.reshape(L, 1, N))


def _proj_kernel(x_ref, mod_ref, g_ref, w_ref, b_ref, wvt_ref, bvt_ref, cos_ref, sin_ref, wsp_ref, bsp_ref,
                 lng_ref, lnb_ref, q_ref, k_ref, vt_ref, hc_ref, tg_ref):
    tm = x_ref.shape[0]
    h = _ada_norm(x_ref[...], g_ref[...], mod_ref[1:2, :], mod_ref[0:1, :]).astype(BF16)

    vt = lax.dot_general(wvt_ref[...], h, (((1,), (1,)), ((), ())), preferred_element_type=F32)
    vt_ref[...] = (vt + bvt_ref[...]).astype(BF16)

    qkv = jnp.dot(h, w_ref[:, 0:2 * QK_COLS], preferred_element_type=F32) + b_ref[:, 0:2 * QK_COLS]
    cos = jnp.concatenate([cos_ref[...]] * (QK_COLS // LANES), axis=1)
    sin = jnp.concatenate([sin_ref[...]] * (QK_COLS // LANES), axis=1)
    lane = lax.broadcasted_iota(jnp.int32, (tm, QK_COLS), 1)
    first_half = (lane & (ATT_HEAD_DIM - 1)) < (ATT_HEAD_DIM // 2)

    def rope(t):
        fwd = pltpu.roll(t, QK_COLS - ATT_HEAD_DIM // 2, axis=1)
        bwd = pltpu.roll(t, ATT_HEAD_DIM // 2, axis=1)
        return t * cos + jnp.where(first_half, fwd, bwd) * sin

    q_ref[...] = (rope(qkv[:, 0:QK_COLS]) * (ATT_HEAD_DIM ** -0.5)).astype(BF16)
    k_ref[...] = rope(qkv[:, QK_COLS:2 * QK_COLS]).astype(BF16)

    c0 = QKV_COLS
    cv = jnp.dot(h, w_ref[:, c0:c0 + CONV_IN_COLS], preferred_element_type=F32) + b_ref[:, c0:c0 + CONV_IN_COLS]
    hc_ref[...] = (cv[:, 0:CONV_CH] * _sigmoid(cv[:, CONV_CH:])).astype(BF16)

    c1 = c0 + CONV_IN_COLS
    z = jax.nn.gelu(jnp.dot(h, w_ref[:, c1:c1 + GMLP_IN_COLS], preferred_element_type=F32)
                    + b_ref[:, c1:c1 + GMLP_IN_COLS])
    u = z[:, 0:GMLP_CH]
    vv = z[:, GMLP_CH:]
    mu = jnp.mean(vv, axis=-1, keepdims=True)
    vc = vv - mu
    var = jnp.mean(vc * vc, axis=-1, keepdims=True)
    vn = (vc * lax.rsqrt(var + EPS) * lng_ref[...] + lnb_ref[...]).astype(BF16)
    row = lax.broadcasted_iota(jnp.int32, (GMLP_BLOCK, GMLP_BLOCK), 0)
    col = lax.broadcasted_iota(jnp.int32, (GMLP_BLOCK, GMLP_BLOCK), 1)
    gc = GMLP_CH // GMLP_GROUPS
    for g in range(GMLP_GROUPS):
        ws = jnp.where(row >= col, wsp_ref[g], 0.0).astype(BF16)
        bias = bsp_ref[g]
        for n in range(tm // GMLP_BLOCK):
            rs = slice(n * GMLP_BLOCK, (n + 1) * GMLP_BLOCK)
            cs = slice(g * gc, (g + 1) * gc)
            sg = jnp.dot(ws, vn[rs, cs], preferred_element_type=F32) + bias
            tg_ref[rs, cs] = (u[rs, cs] * sg).astype(BF16)


def _proj(x, mod, ln_g, w, b, wvt, bvt, cos_t, sin_t, w_sp, b_sp_t, gln_g, gln_b, tm):
    B, S, D = x.shape
    tok = lambda n: pl.BlockSpec((None, tm, n), lambda bi, i: (bi, i, 0))
    tok_out = jax.ShapeDtypeStruct((B, S, 512), BF16)
    vt_out = jax.ShapeDtypeStruct((B, S // tm, ATT_WIDTH, tm), BF16)
    vt_spec = pl.BlockSpec((None, None, ATT_WIDTH, tm), lambda bi, i: (bi, i, 0, 0))
    return pl.pallas_call(
        _proj_kernel,
        grid=(B, S // tm),
        in_specs=[tok(D),
                  pl.BlockSpec((None, 6, D), lambda bi, i: (bi, 0, 0)),
                  _const_spec((1, D)),
                  _const_spec((D, BRANCH_COLS)),
                  _const_spec((1, BRANCH_COLS)),
                  _const_spec((ATT_WIDTH, D)), _const_spec((ATT_WIDTH, 1)),
                  tok(LANES), tok(LANES),
                  _const_spec((GMLP_GROUPS, GMLP_BLOCK, GMLP_BLOCK)),
                  _const_spec((GMLP_GROUPS, GMLP_BLOCK, GMLP_BLOCK)),
                  _const_spec((1, GMLP_CH)), _const_spec((1, GMLP_CH))],
        out_specs=[tok(512), tok(512), vt_spec, tok(512), tok(512)],
        out_shape=[tok_out, tok_out, vt_out, tok_out, tok_out],
        compiler_params=_params(("parallel", "parallel")),
        name="in_proj",
    )(x, mod, ln_g, w, b, wvt, bvt, cos_t, sin_t, w_sp, b_sp_t, gln_g, gln_b)


def _col_max(s):
    r = s.shape[0]
    while r > SUBLANES and r % 2 == 0:
        r //= 2
        s = jnp.maximum(s[:r], s[r:])
    return jnp.max(s, axis=0, keepdims=True)


def _attn_scores(q_ref, k_ref, i, j, s_ref, mx_ref, t):
    q = q_ref[pl.ds(pl.multiple_of(i * t, t), t), :]
    lane = lax.broadcasted_iota(jnp.int32, q.shape, 1)
    zero = jnp.zeros_like(q)
    qs = jnp.concatenate([jnp.where(lane < ATT_HEAD_DIM, q, zero),
                          jnp.where(lane >= ATT_HEAD_DIM, q, zero)], axis=0)
    kj = k_ref[pl.ds(pl.multiple_of(j * t, t), t), :]
    s = lax.dot_general(kj, qs, (((1,), (1,)), ((), ())), preferred_element_type=F32)
    s_ref[...] = s
    mx_ref[...] = _col_max(s)


def _attn_q_tile(i, lq1_ref, lk1_ref, lq2_ref, lk2_ref, q_ref, k_ref, vt_ref, g_ref, o_ref,
                 m_ref, acc_ref, sa_ref, sb_ref, mxa_ref, mxb_ref, *, t, lam_init):
    nq = q_ref.shape[0] // t
    m_ref[...] = jnp.full_like(m_ref, NEG_INF)
    acc_ref[...] = jnp.zeros_like(acc_ref)
    ones = jnp.ones((SUM_ROWS, t), BF16)

    def scores(j, s_ref, mx_ref):
        _attn_scores(q_ref, k_ref, i, j, s_ref, mx_ref, t)

    def accumulate(j, s_ref, mx_ref, masked):
        s = s_ref[...]
        if masked:
            ck = lax.broadcasted_iota(jnp.int32, (t, t), 0)
            rq = lax.broadcasted_iota(jnp.int32, (t, t), 1)
            keep = (rq >> CHUNK_SHIFT) >= (ck >> CHUNK_SHIFT)
            s = jnp.where(jnp.concatenate([keep, keep], axis=1), s, NEG_INF)
            tile_max = _col_max(s)
        else:
            tile_max = mx_ref[...]
        m_old = m_ref[...]
        m_new = jnp.maximum(m_old, tile_max)
        p = jnp.exp(s - m_new).astype(BF16)
        v1 = jnp.concatenate([vt_ref[j], ones], axis=0)
        acc_ref[...] = jnp.exp(m_old - m_new) * acc_ref[...] + jnp.dot(v1, p, preferred_element_type=F32)
        m_ref[...] = m_new

    a, b = (sa_ref, mxa_ref), (sb_ref, mxb_ref)

    def body(jj, carry):
        j = 2 * jj
        scores(j + 1, *b)
        accumulate(j, *a, False)
        scores(j + 2, *a)
        accumulate(j + 1, *b, False)
        return carry

    lax.fori_loop(0, i // 2, body, 0)

    @pl.when(i % 2 == 1)
    def _():
        scores(i, *b)
        accumulate(i - 1, *a, False)
        accumulate(i, *b, True)

    @pl.when(i % 2 == 0)
    def _():
        accumulate(i, *a, True)

    _attn_scores(q_ref, k_ref, jnp.minimum(i + 1, nq - 1), 0, *a, t)

    lam = (jnp.exp(jnp.sum(lq1_ref[...] * lk1_ref[...], axis=-1, keepdims=True))
           - jnp.exp(jnp.sum(lq2_ref[...] * lk2_ref[...], axis=-1, keepdims=True)) + lam_init)
    o_all = acc_ref[0:ATT_V_DIM, :] / acc_ref[ATT_V_DIM:ATT_V_DIM + 1, :]
    o = (o_all[:, 0:t] - lam * o_all[:, t:2 * t]).T
    r = lax.rsqrt(jnp.mean(o * o, axis=-1, keepdims=True) + EPS)
    o_ref[pl.ds(pl.multiple_of(i * t, t), t), :] = (o * r * g_ref[...] * (1.0 - lam_init)).astype(BF16)


def _attn_kernel(lq1_ref, lk1_ref, lq2_ref, lk2_ref, q_ref, k_ref, vt_ref, g_ref, o_ref,
                 m_ref, acc_ref, sa_ref, sb_ref, mxa_ref, mxb_ref, *, t, lam_init):
    _attn_scores(q_ref, k_ref, 0, 0, sa_ref, mxa_ref, t)

    def q_tile(i, carry):
        _attn_q_tile(i, lq1_ref, lk1_ref, lq2_ref, lk2_ref, q_ref, k_ref, vt_ref, g_ref, o_ref,
                     m_ref, acc_ref, sa_ref, sb_ref, mxa_ref, mxb_ref, t=t, lam_init=lam_init)
        return carry

    lax.fori_loop(0, q_ref.shape[0] // t, q_tile, 0)


def _attention(q, k, vt, lq1, lk1, lq2, lk2, subln_g, lam_init, t):
    B, S, _ = q.shape
    lam_spec = _const_spec((1, ATT_HEAD_DIM))
    qko_spec = pl.BlockSpec((None, S, ATT_V_DIM), lambda b, h: (b, 0, h))
    vt_spec = pl.BlockSpec((None, S // t, ATT_V_DIM, t), lambda b, h: (b, 0, h, 0))
    return pl.pallas_call(
        functools.partial(_attn_kernel, t=t, lam_init=lam_init),
        grid=(B, ATT_HEADS),
        in_specs=[lam_spec, lam_spec, lam_spec, lam_spec, qko_spec, qko_spec, vt_spec,
                  _const_spec((1, ATT_V_DIM))],
        out_specs=qko_spec,
        out_shape=jax.ShapeDtypeStruct((B, S, ATT_WIDTH), BF16),
        scratch_shapes=[pltpu.VMEM((1, 2 * t), F32), pltpu.VMEM((ATT_V_DIM + SUM_ROWS, 2 * t), F32),
                        pltpu.VMEM((t, 2 * t), F32), pltpu.VMEM((t, 2 * t), F32),
                        pltpu.VMEM((1, 2 * t), F32), pltpu.VMEM((1, 2 * t), F32)],
        compiler_params=_params(("parallel", "parallel")),
        name="diff_attn",
    )(lq1, lk1, lq2, lk2, q, k, vt, subln_g)


def _merge_kernel(x_ref, mod_ref, g_ref, wg_ref, bg_ref, o_ref, wa_ref, hcp_ref, hc_ref, cw_ref, cb_ref,
                  clg_ref, clb_ref, wc_ref, tg_ref, wm_ref, wo_ref, y_ref, ext_ref, cn_ref):
    tm = x_ref.shape[0]
    i = pl.program_id(1)
    x = x_ref[...]
    h = _ada_norm(x, g_ref[...], mod_ref[1:2, :], mod_ref[0:1, :]).astype(BF16)

    def gated(branch, y):
        cols = slice(branch * D_MODEL, (branch + 1) * D_MODEL)
        gate = _sigmoid(jnp.dot(h, wg_ref[:, cols], preferred_element_type=F32) + bg_ref[:, cols])
        return gate * y

    mixed = gated(0, jnp.dot(o_ref[...], wa_ref[...], preferred_element_type=F32))
    mixed = mixed + gated(2, jnp.dot(tg_ref[...], wm_ref[...], preferred_element_type=F32))

    prev = hcp_ref[...].astype(F32)
    ext_ref[0, 0:CONV_HALO, :] = jnp.where(i > 0, prev, jnp.zeros_like(prev))
    ext_ref[0, CONV_HALO:, :] = hc_ref[...].astype(F32)
    n_sh = tm + CONV_HALO - SUBLANES
    for s in range(1, SUBLANES):
        ext_ref[s, 0:n_sh, :] = ext_ref[0, s:s + n_sh, :]
    rb = 32
    first = CONV_HALO - (CONV_WIDTH - 1)

    for base in range(0, tm, rb):
        acc = jnp.zeros((rb, CONV_CH), F32) + cb_ref[...]
        for j in range(CONV_WIDTH):
            s = (first + j) % SUBLANES
            r0 = base + (first + j - s)
            acc = acc + ext_ref[s, r0:r0 + rb, :] * jnp.tile(cw_ref[j], (rb // SUBLANES, 1))
        mu = jnp.mean(acc, axis=-1, keepdims=True)
        ac = acc - mu
        var = jnp.mean(ac * ac, axis=-1, keepdims=True)
        yn = ac * lax.rsqrt(var + EPS) * clg_ref[...] + clb_ref[...]
        cn_ref[base:base + rb, :] = (yn * _sigmoid(yn)).astype(BF16)

    mixed = mixed + gated(1, jnp.dot(cn_ref[...], wc_ref[...], preferred_element_type=F32))
    y_ref[...] = x + mod_ref[2:3, :] * jnp.dot(mixed.astype(BF16), wo_ref[...], preferred_element_type=F32)


def _merge(x, mod, ln_g, w_gate, b_gate, o, w_att, hc, cw, cb, clg, clb, w_conv, tg, w_gmlp, w_o, tm):
    B, S, D = x.shape
    tok = lambda n: pl.BlockSpec((None, tm, n), lambda bi, i: (bi, i, 0))
    halo = pl.BlockSpec((None, CONV_HALO, CONV_CH),
                        lambda bi, i: (bi, jnp.maximum(i * (tm // CONV_HALO) - 1, 0), 0))
    return pl.pallas_call(
        _merge_kernel,
        grid=(B, S // tm),
        in_specs=[tok(D),
                  pl.BlockSpec((None, 6, D), lambda bi, i: (bi, 0, 0)),
                  _const_spec((1, D)),
                  _const_spec((D, GATE_COLS)), _const_spec((1, GATE_COLS)),
                  tok(ATT_WIDTH), _const_spec((ATT_WIDTH, D)),
                  halo, tok(CONV_CH),
                  _const_spec((CONV_WIDTH, SUBLANES, CONV_CH)), _const_spec((1, CONV_CH)),
                  _const_spec((1, CONV_CH)), _const_spec((1, CONV_CH)),
                  _const_spec((CONV_CH, D)),
                  tok(GMLP_CH), _const_spec((GMLP_CH, D)),
                  _const_spec((D, D))],
        out_specs=tok(D),
        out_shape=jax.ShapeDtypeStruct((B, S, D), F32),
        scratch_shapes=[pltpu.VMEM((SUBLANES, CONV_HALO + tm, CONV_CH), F32), pltpu.VMEM((tm, CONV_CH), BF16)],
        compiler_params=_params(("parallel", "parallel")),
        name="merge",
    )(x, mod, ln_g, w_gate, b_gate, o, w_att, hc, hc, cw, cb, clg, clb, w_conv, tg, w_gmlp, w_o)


def _ffn_kernel(x_ref, mod_ref, g_ref, wu_ref, cw_ref, cb_ref, wd_ref, fg_ref, y_ref,
                h_ref, exta_ref, extb_ref, carry_ref, acc_ref, *, final):
    tm = x_ref.shape[0]
    nc, _, fc2 = wu_ref.shape
    fc = fc2 // 2
    i = pl.program_id(1)
    h_ref[...] = _ada_norm(x_ref[...], g_ref[...], mod_ref[4:5, :], mod_ref[3:4, :]).astype(BF16)
    acc_ref[...] = jnp.zeros_like(acc_ref)

    @pl.when(i == 0)
    def _():
        carry_ref[...] = jnp.zeros_like(carry_ref)

    def up_proj(c, ext_ref):
        up = jnp.dot(h_ref[...], wu_ref[c], preferred_element_type=F32)
        ext_ref[0:SUBLANES, :] = carry_ref[c]
        ext_ref[SUBLANES:, :] = up
        carry_ref[c] = up[tm - SUBLANES:, :]

    def down_proj(c, ext_ref):
        cw = cw_ref[c]
        out = cb_ref[c] + ext_ref[SUBLANES:, :] * cw[2:3, :]
        out = out + ext_ref[SUBLANES - 1:SUBLANES - 1 + tm, :] * cw[1:2, :]
        out = out + ext_ref[SUBLANES - 2:SUBLANES - 2 + tm, :] * cw[0:1, :]
        gate = out[:, 0:fc]
        act = (gate * _sigmoid(gate) * out[:, fc:]).astype(BF16)
        acc_ref[...] += jnp.dot(act, wd_ref[c], preferred_element_type=F32)

    assert nc % 2 == 1
    up_proj(0, exta_ref)

    def pair(jj, carry):
        c = 2 * jj
        up_proj(c + 1, extb_ref)
        down_proj(c, exta_ref)
        up_proj(c + 2, exta_ref)
        down_proj(c + 1, extb_ref)
        return carry

    lax.fori_loop(0, nc // 2, pair, 0)
    down_proj(nc - 1, exta_ref)
    y = x_ref[...] + mod_ref[5:6, :] * acc_ref[...]
    if final:
        r = lax.rsqrt(jnp.mean(y * y, axis=-1, keepdims=True) + EPS)
        y = y * r * fg_ref[...]
    y_ref[...] = y


def _ffn(x, mod, ln_g, w_up_c, cw_c, cb_c, w_down_c, final_g, tm, final):
    B, S, D = x.shape
    nc, _, fc2 = w_up_c.shape
    tok = pl.BlockSpec((None, tm, D), lambda bi, i: (bi, i, 0))
    return pl.pallas_call(
        functools.partial(_ffn_kernel, final=final),
        grid=(B, S // tm),
        in_specs=[tok,
                  pl.BlockSpec((None, 6, D), lambda bi, i: (bi, 0, 0)),
                  _const_spec((1, D)),
                  _const_spec((nc, D, fc2)),
                  _const_spec((nc, FFN_CONV_WIDTH, fc2)), _const_spec((nc, 1, fc2)),
                  _const_spec((nc, fc2 // 2, D)),
                  _const_spec((1, D))],
        out_specs=tok,
        out_shape=jax.ShapeDtypeStruct((B, S, D), F32),
        scratch_shapes=[pltpu.VMEM((tm, D), BF16),
                        pltpu.VMEM((SUBLANES + tm, fc2), F32), pltpu.VMEM((SUBLANES + tm, fc2), F32),
                        pltpu.VMEM((nc, SUBLANES, fc2), F32), pltpu.VMEM((tm, D), F32)],
        compiler_params=_params(("parallel", "arbitrary")),
        name="ffn",
    )(x, mod, ln_g, w_up_c, cw_c, cb_c, w_down_c, final_g)


def _chunk_cols(a, fc):
    lead = a.shape[:-1]
    g = a[..., :FFN_DIM].reshape(*lead, FFN_DIM // fc, fc)
    v = a[..., FFN_DIM:].reshape(*lead, FFN_DIM // fc, fc)
    return jnp.moveaxis(jnp.concatenate([g, v], axis=-1), -2, 0)


def _tile(s, pref):
    return pref if s % pref == 0 else s


def kernel(x, c, positions, ln1_g, ln2_g, w_ada, b_ada, w_in, b_in, lambda_q1, lambda_k1, lambda_q2, lambda_k2,
           attn_subln_g, w_attn_out, conv_dw_w, conv_dw_b, conv_ln_g, conv_ln_b, w_conv_out, gmlp_ln_g, gmlp_ln_b,
           w_spatial, b_spatial, w_gmlp_out, w_o, w_up, ffn_dw_w, ffn_dw_b, w_down, final_g):
    B, S, D = x.shape
    L = w_in.shape[0]
    tm = _tile(S, TOKEN_TILE)
    ta = tm

    inv_freq = 1.0 / (ROPE_THETA ** (jnp.arange(0, ATT_HEAD_DIM, 2, dtype=F32) / ATT_HEAD_DIM))
    ang = positions.astype(F32)[..., None] * inv_freq
    cos, sin = jnp.cos(ang), jnp.sin(ang)
    cos_t = jnp.concatenate([cos, cos, cos, cos], axis=-1)
    sin_t = jnp.concatenate([-sin, sin, -sin, sin], axis=-1)

    mod = _modulation(c, w_ada, b_ada).reshape(L, B, 6, D)
    row = lambda a: a.reshape(1, -1)

    for l in range(L):
        lam_init = 0.8 - 0.6 * math.exp(-0.3 * l)
        w_l = w_in[l].astype(BF16)
        q, k, vt, hc, tg = _proj(x, mod[l], row(ln1_g[l]), w_l[:, :BRANCH_COLS], row(b_in[l, :BRANCH_COLS]),
                                w_l[:, 2 * QK_COLS:QKV_COLS].T, b_in[l, 2 * QK_COLS:QKV_COLS].reshape(-1, 1),
                                cos_t, sin_t, w_spatial[l],
                                jnp.broadcast_to(b_spatial[l][:, :, None], w_spatial[l].shape),
                                row(gmlp_ln_g[l]), row(gmlp_ln_b[l]), tm)
        o = _attention(q, k, vt, row(lambda_q1[l]), row(lambda_k1[l]), row(lambda_q2[l]), row(lambda_k2[l]),
                       row(attn_subln_g[l]), lam_init, ta)
        x = _merge(x, mod[l], row(ln1_g[l]), w_l[:, BRANCH_COLS:], row(b_in[l, BRANCH_COLS:]),
                   o, w_attn_out[l].astype(BF16), hc,
                   jnp.broadcast_to(conv_dw_w[l][:, None, :], (CONV_WIDTH, SUBLANES, CONV_CH)), row(conv_dw_b[l]),
                   row(conv_ln_g[l]), row(conv_ln_b[l]), w_conv_out[l].astype(BF16),
                   tg, w_gmlp_out[l].astype(BF16), w_o[l].astype(BF16), tm)
        x = _ffn(x, mod[l], row(ln2_g[l]), _chunk_cols(w_up[l].astype(BF16), FFN_CHUNK),
                 _chunk_cols(ffn_dw_w[l], FFN_CHUNK), _chunk_cols(row(ffn_dw_b[l]), FFN_CHUNK),
                 w_down[l].astype(BF16).reshape(FFN_DIM // FFN_CHUNK, FFN_CHUNK, D), row(final_g),
                 tm, l == L - 1)
    return x
```

```python
import functools
import math

import jax
import jax.numpy as jnp
from jax import lax
from jax.experimental import pallas as pl
from jax.experimental.pallas import tpu as pltpu

D_MODEL = 1024
CHUNK = 64
ATT_HEADS = 4
ATT_HEAD_DIM = 64
ATT_V_DIM = 128
ATT_WIDTH = 512
QK_COLS = 512
ROPE_THETA = 10000.0
CONV_CH = 512
CONV_WIDTH = 31
GMLP_CH = 512
GMLP_GROUPS = 4
GMLP_BLOCK = 128
FFN_DIM = 2816
FFN_CONV_WIDTH = 3
EPS = 1e-6
QKV_COLS = 2 * QK_COLS + ATT_WIDTH
CONV_IN_COLS = 2 * CONV_CH
GMLP_IN_COLS = 2 * GMLP_CH
BRANCH_COLS = QKV_COLS + CONV_IN_COLS + GMLP_IN_COLS
GATE_COLS = 3 * D_MODEL

LANES = 128
SUBLANES = 8
CONV_HALO = 32
VMEM_LIMIT = 56 * 1024 * 1024
NEG_INF = -1e30
CHUNK_SHIFT = CHUNK.bit_length() - 1
TOKEN_TILE = 512
SUM_ROWS = 16
FFN_CHUNK = 256

F32 = jnp.float32
BF16 = jnp.bfloat16


def _const_spec(shape):
    n = len(shape)
    return pl.BlockSpec(shape, lambda *_: (0,) * n, pipeline_mode=pl.Buffered(1))


def _params(sem):
    return pltpu.CompilerParams(dimension_semantics=sem, vmem_limit_bytes=VMEM_LIMIT)


def _sigmoid(x):
    return 0.5 * jnp.tanh(0.5 * x) + 0.5


def _ada_norm(x, g, scale, shift):
    r = lax.rsqrt(jnp.mean(x * x, axis=-1, keepdims=True) + EPS)
    return x * r * (g * (1.0 + scale)) + shift


def _mod_kernel(c_ref, w_ref, b_ref, o_ref):
    c = c_ref[...]
    ca = (c * _sigmoid(c)).astype(BF16)
    o_ref[...] = jnp.dot(ca, w_ref[...].astype(BF16), preferred_element_type=F32) + b_ref[...]


def _modulation(c, w_ada, b_ada):
    L, D, N = w_ada.shape
    B = c.shape[0]
    tn = 1536
    return pl.pallas_call(
        _mod_kernel,
        grid=(L, N // tn),
        in_specs=[pl.BlockSpec((B, D), lambda l, j: (0, 0)),
                  pl.BlockSpec((None, D, tn), lambda l, j: (l, 0, j)),
                  pl.BlockSpec((None, 1, tn), lambda l, j: (l, 0, j))],
        out_specs=pl.BlockSpec((None, B, tn), lambda l, j: (l, 0, j)),
        out_shape=jax.ShapeDtypeStruct((L, B, N), F32),
        compiler_params=_params(("parallel", "parallel")),
        name="adaln_mod",
    )(c, w_ada, b_ada.reshape(L, 1, N))


def _proj_kernel(x_ref, mod_ref, g_ref, w_ref, b_ref, wvt_ref, bvt_ref, cos_ref, sin_ref, wsp_ref, bsp_ref,
                 lng_ref, lnb_ref, q_ref, k_ref, vt_ref, hc_ref, tg_ref):
    tm = x_ref.shape[0]
    h = _ada_norm(x_ref[...], g_ref[...], mod_ref[1:2, :], mod_ref[0:1, :]).astype(BF16)

    vt = lax.dot_general(wvt_ref[...], h, (((1,), (1,)), ((), ())), preferred_element_type=F32)
    vt_ref[...] = (vt + bvt_ref[...]).astype(BF16)

    qkv = jnp.dot(h, w_ref[:, 0:2 * QK_COLS], preferred_element_type=F32) + b_ref[:, 0:2 * QK_COLS]
    cos = jnp.concatenate([cos_ref[...]] * (QK_COLS // LANES), axis=1)
    sin = jnp.concatenate([sin_ref[...]] * (QK_COLS // LANES), axis=1)
    lane = lax.broadcasted_iota(jnp.int32, (tm, QK_COLS), 1)
    first_half = (lane & (ATT_HEAD_DIM - 1)) < (ATT_HEAD_DIM // 2)

    def rope(t):
        fwd = pltpu.roll(t, QK_COLS - ATT_HEAD_DIM // 2, axis=1)
        bwd = pltpu.roll(t, ATT_HEAD_DIM // 2, axis=1)
        return t * cos + jnp.where(first_half, fwd, bwd) * sin

    q_ref[...] = (rope(qkv[:, 0:QK_COLS]) * (ATT_HEAD_DIM ** -0.5)).astype(BF16)
    k_ref[...] = rope(qkv[:, QK_COLS:2 * QK_COLS]).astype(BF16)

    c0 = QKV_COLS
    cv = jnp.dot(h, w_ref[:, c0:c0 + CONV_IN_COLS], preferred_element_type=F32) + b_ref[:, c0:c0 + CONV_IN_COLS]
    hc_ref[...] = (cv[:, 0:CONV_CH] * _sigmoid(cv[:, CONV_CH:])).astype(BF16)

    c1 = c0 + CONV_IN_COLS
    z = jax.nn.gelu(jnp.dot(h, w_ref[:, c1:c1 + GMLP_IN_COLS], preferred_element_type=F32)
                    + b_ref[:, c1:c1 + GMLP_IN_COLS])
    u = z[:, 0:GMLP_CH]
    vv = z[:, GMLP_CH:]
    mu = jnp.mean(vv, axis=-1, keepdims=True)
    vc = vv - mu
    var = jnp.mean(vc * vc, axis=-1, keepdims=True)
    vn = (vc * lax.rsqrt(var + EPS) * lng_ref[...] + lnb_ref[...]).astype(BF16)
    row = lax.broadcasted_iota(jnp.int32, (GMLP_BLOCK, GMLP_BLOCK), 0)
    col = lax.broadcasted_iota(jnp.int32, (GMLP_BLOCK, GMLP_BLOCK), 1)
    gc = GMLP_CH // GMLP_GROUPS
    for g in range(GMLP_GROUPS):
        ws = jnp.where(row >= col, wsp_ref[g], 0.0).astype(BF16)
        bias = bsp_ref[g]
        for n in range(tm // GMLP_BLOCK):
            rs = slice(n * GMLP_BLOCK, (n + 1) * GMLP_BLOCK)
            cs = slice(g * gc, (g + 1) * gc)
            sg = jnp.dot(ws, vn[rs, cs], preferred_element_type=F32) + bias
            tg_ref[rs, cs] = (u[rs, cs] * sg).astype(BF16)


def _proj(x, mod, ln_g, w, b, wvt, bvt, cos_t, sin_t, w_sp, b_sp_t, gln_g, gln_b, tm):
    B, S, D = x.shape
    tok = lambda n: pl.BlockSpec((None, tm, n), lambda bi, i: (bi, i, 0))
    tok_out = jax.ShapeDtypeStruct((B, S, 512), BF16)
    vt_out = jax.ShapeDtypeStruct((B, S // tm, ATT_WIDTH, tm), BF16)
    vt_spec = pl.BlockSpec((None, None, ATT_WIDTH, tm), lambda bi, i: (bi, i, 0, 0))
    return pl.pallas_call(
        _proj_kernel,
        grid=(B, S // tm),
        in_specs=[tok(D),
                  pl.BlockSpec((None, 6, D), lambda bi, i: (bi, 0, 0)),
                  _const_spec((1, D)),
                  _const_spec((D, BRANCH_COLS)),
                  _const_spec((1, BRANCH_COLS)),
                  _const_spec((ATT_WIDTH, D)), _const_spec((ATT_WIDTH, 1)),
                  tok(LANES), tok(LANES),
                  _const_spec((GMLP_GROUPS, GMLP_BLOCK, GMLP_BLOCK)),
                  _const_spec((GMLP_GROUPS, GMLP_BLOCK, GMLP_BLOCK)),
                  _const_spec((1, GMLP_CH)), _const_spec((1, GMLP_CH))],
        out_specs=[tok(512), tok(512), vt_spec, tok(512), tok(512)],
        out_shape=[tok_out, tok_out, vt_out, tok_out, tok_out],
        compiler_params=_params(("parallel", "parallel")),
        name="in_proj",
    )(x, mod, ln_g, w, b, wvt, bvt, cos_t, sin_t, w_sp, b_sp_t, gln_g, gln_b)


def _col_max(s):
    r = s.shape[0]
    while r > SUBLANES and r % 2 == 0:
        r //= 2
        s = jnp.maximum(s[:r], s[r:])
    return jnp.max(s, axis=0, keepdims=True)


def _attn_scores(q_ref, k_ref, i, j, s_ref, mx_ref, t):
    q = q_ref[pl.ds(pl.multiple_of(i * t, t), t), :]
    lane = lax.broadcasted_iota(jnp.int32, q.shape, 1)
    zero = jnp.zeros_like(q)
    qs = jnp.concatenate([jnp.where(lane < ATT_HEAD_DIM, q, zero),
                          jnp.where(lane >= ATT_HEAD_DIM, q, zero)], axis=0)
    kj = k_ref[pl.ds(pl.multiple_of(j * t, t), t), :]
    s = lax.dot_general(kj, qs, (((1,), (1,)), ((), ())), preferred_element_type=F32)
    s_ref[...] = s
    mx_ref[...] = _col_max(s)


def _attn_q_tile(i, lq1_ref, lk1_ref, lq2_ref, lk2_ref, q_ref, k_ref, vt_ref, g_ref, o_ref,
                 m_ref, acc_ref, sa_ref, sb_ref, mxa_ref, mxb_ref, *, t, lam_init):
    nq = q_ref.shape[0] // t
    m_ref[...] = jnp.full_like(m_ref, NEG_INF)
    acc_ref[...] = jnp.zeros_like(acc_ref)
    ones = jnp.ones((SUM_ROWS, t), BF16)

    def scores(j, s_ref, mx_ref):
        _attn_scores(q_ref, k_ref, i, j, s_ref, mx_ref, t)

    def accumulate(j, s_ref, mx_ref, masked):
        s = s_ref[...]
        if masked:
            ck = lax.broadcasted_iota(jnp.int32, (t, t), 0)
            rq = lax.broadcasted_iota(jnp.int32, (t, t), 1)
            keep = (rq >> CHUNK_SHIFT) >= (ck >> CHUNK_SHIFT)
            s = jnp.where(jnp.concatenate([keep, keep], axis=1), s, NEG_INF)
            tile_max = _col_max(s)
        else:
            tile_max = mx_ref[...]
        m_old = m_ref[...]
        m_new = jnp.maximum(m_old, tile_max)
        p = jnp.exp(s - m_new).astype(BF16)
        v1 = jnp.concatenate([vt_ref[j], ones], axis=0)
        acc_ref[...] = jnp.exp(m_old - m_new) * acc_ref[...] + jnp.dot(v1, p, preferred_element_type=F32)
        m_ref[...] = m_new

    a, b = (sa_ref, mxa_ref), (sb_ref, mxb_ref)

    def body(jj, carry):
        j = 2 * jj
        scores(j + 1, *b)
        accumulate(j, *a, False)
        scores(j + 2, *a)
        accumulate(j + 1, *b, False)
        return carry

    lax.fori_loop(0, i // 2, body, 0)

    @pl.when(i % 2 == 1)
    def _():
        scores(i, *b)
        accumulate(i - 1, *a, False)
        accumulate(i, *b, True)

    @pl.when(i % 2 == 0)
    def _():
        accumulate(i, *a, True)

    _attn_scores(q_ref, k_ref, jnp.minimum(i + 1, nq - 1), 0, *a, t)

    lam = (jnp.exp(jnp.sum(lq1_ref[...] * lk1_ref[...], axis=-1, keepdims=True))
           - jnp.exp(jnp.sum(lq2_ref[...] * lk2_ref[...], axis=-1, keepdims=True)) + lam_init)
    o_all = acc_ref[0:ATT_V_DIM, :] / acc_ref[ATT_V_DIM:ATT_V_DIM + 1, :]
    o = (o_all[:, 0:t] - lam * o_all[:, t:2 * t]).T
    r = lax.rsqrt(jnp.mean(o * o, axis=-1, keepdims=True) + EPS)
    o_ref[pl.ds(pl.multiple_of(i * t, t), t), :] = (o * r * g_ref[...] * (1.0 - lam_init)).astype(BF16)


def _attn_kernel(lq1_ref, lk1_ref, lq2_ref, lk2_ref, q_ref, k_ref, vt_ref, g_ref, o_ref,
                 m_ref, acc_ref, sa_ref, sb_ref, mxa_ref, mxb_ref, *, t, lam_init):
    _attn_scores(q_ref, k_ref, 0, 0, sa_ref, mxa_ref, t)

    def q_tile(i, carry):
        _attn_q_tile(i, lq1_ref, lk1_ref, lq2_ref, lk2_ref, q_ref, k_ref, vt_ref, g_ref, o_ref,
                     m_ref, acc_ref, sa_ref, sb_ref, mxa_ref, mxb_ref, t=t, lam_init=lam_init)
        return carry

    lax.fori_loop(0, q_ref.shape[0] // t, q_tile, 0)


def _attention(q, k, vt, lq1, lk1, lq2, lk2, subln_g, lam_init, t):
    B, S, _ = q.shape
    lam_spec = _const_spec((1, ATT_HEAD_DIM))
    qko_spec = pl.BlockSpec((None, S, ATT_V_DIM), lambda b, h: (b, 0, h))
    vt_spec = pl.BlockSpec((None, S // t, ATT_V_DIM, t), lambda b, h: (b, 0, h, 0))
    return pl.pallas_call(
        functools.partial(_attn_kernel, t=t, lam_init=lam_init),
        grid=(B, ATT_HEADS),
        in_specs=[lam_spec, lam_spec, lam_spec, lam_spec, qko_spec, qko_spec, vt_spec,
                  _const_spec((1, ATT_V_DIM))],
        out_specs=qko_spec,
        out_shape=jax.ShapeDtypeStruct((B, S, ATT_WIDTH), BF16),
        scratch_shapes=[pltpu.VMEM((1, 2 * t), F32), pltpu.VMEM((ATT_V_DIM + SUM_ROWS, 2 * t), F32),
                        pltpu.VMEM((t, 2 * t), F32), pltpu.VMEM((t, 2 * t), F32),
                        pltpu.VMEM((1, 2 * t), F32), pltpu.VMEM((1, 2 * t), F32)],
        compiler_params=_params(("parallel", "parallel")),
        name="diff_attn",
    )(lq1, lk1, lq2, lk2, q, k, vt, subln_g)


def _merge_kernel(x_ref, mod_ref, g_ref, wg_ref, bg_ref, o_ref, wa_ref, hcp_ref, hc_ref, cw_ref, cb_ref,
                  clg_ref, clb_ref, wc_ref, tg_ref, wm_ref, wo_ref, y_ref, ext_ref, cn_ref):
    tm = x_ref.shape[0]
    i = pl.program_id(1)
    x = x_ref[...]
    h = _ada_norm(x, g_ref[...], mod_ref[1:2, :], mod_ref[0:1, :]).astype(BF16)

    def gated(branch, y_half):
        cols = slice(branch * D_MODEL, (branch + 1) * D_MODEL)
        z_half = jnp.dot(h, wg_ref[:, cols], preferred_element_type=F32) + bg_ref[:, cols]
        return (jnp.tanh(z_half) + 1.0) * y_half

    mixed = gated(0, jnp.dot(o_ref[...], wa_ref[...], preferred_element_type=F32))
    mixed = mixed + gated(2, jnp.dot(tg_ref[...], wm_ref[...], preferred_element_type=F32))

    prev = hcp_ref[...].astype(F32)
    ext_ref[0, 0:CONV_HALO, :] = jnp.where(i > 0, prev, jnp.zeros_like(prev))
    ext_ref[0, CONV_HALO:, :] = hc_ref[...].astype(F32)
    n_sh = tm + CONV_HALO - SUBLANES
    for s in range(1, SUBLANES):
        ext_ref[s, 0:n_sh, :] = ext_ref[0, s:s + n_sh, :]
    rb = 32
    first = CONV_HALO - (CONV_WIDTH - 1)

    for base in range(0, tm, rb):
        acc = jnp.zeros((rb, CONV_CH), F32) + cb_ref[...]
        for j in range(CONV_WIDTH):
            s = (first + j) % SUBLANES
            r0 = base + (first + j - s)
            acc = acc + ext_ref[s, r0:r0 + rb, :] * jnp.tile(cw_ref[j], (rb // SUBLANES, 1))
        mu = jnp.mean(acc, axis=-1, keepdims=True)
        ac = acc - mu
        var = jnp.mean(ac * ac, axis=-1, keepdims=True)
        yn = ac * lax.rsqrt(var + EPS) * clg_ref[...] + clb_ref[...]
        cn_ref[base:base + rb, :] = (yn * _sigmoid(yn)).astype(BF16)

    mixed = mixed + gated(1, jnp.dot(cn_ref[...], wc_ref[...], preferred_element_type=F32))
    y_ref[...] = x + mod_ref[2:3, :] * jnp.dot(mixed.astype(BF16), wo_ref[...], preferred_element_type=F32)


def _merge(x, mod, ln_g, w_gate, b_gate, o, w_att, hc, cw, cb, clg, clb, w_conv, tg, w_gmlp, w_o, tm):
    B, S, D = x.shape
    tok = lambda n: pl.BlockSpec((None, tm, n), lambda bi, i: (bi, i, 0))
    halo = pl.BlockSpec((None, CONV_HALO, CONV_CH),
                        lambda bi, i: (bi, jnp.maximum(i * (tm // CONV_HALO) - 1, 0), 0))
    return pl.pallas_call(
        _merge_kernel,
        grid=(B, S // tm),
        in_specs=[tok(D),
                  pl.BlockSpec((None, 6, D), lambda bi, i: (bi, 0, 0)),
                  _const_spec((1, D)),
                  _const_spec((D, GATE_COLS)), _const_spec((1, GATE_COLS)),
                  tok(ATT_WIDTH), _const_spec((ATT_WIDTH, D)),
                  halo, tok(CONV_CH),
                  _const_spec((CONV_WIDTH, SUBLANES, CONV_CH)), _const_spec((1, CONV_CH)),
                  _const_spec((1, CONV_CH)), _const_spec((1, CONV_CH)),
                  _const_spec((CONV_CH, D)),
                  tok(GMLP_CH), _const_spec((GMLP_CH, D)),
                  _const_spec((D, D))],
        out_specs=tok(D),
        out_shape=jax.ShapeDtypeStruct((B, S, D), F32),
        scratch_shapes=[pltpu.VMEM((SUBLANES, CONV_HALO + tm, CONV_CH), F32), pltpu.VMEM((tm, CONV_CH), BF16)],
        compiler_params=_params(("parallel", "parallel")),
        name="merge",
    )(x, mod, ln_g, w_gate, b_gate, o, w_att, hc, hc, cw, cb, clg, clb, w_conv, tg, w_gmlp, w_o)


def _ffn_kernel(x_ref, mod_ref, g_ref, wu_ref, cw_ref, cb_ref, wd_ref, fg_ref, y_ref,
                h_ref, exta_ref, extb_ref, carry_ref, acc_ref, *, final):
    tm = x_ref.shape[0]
    nc, _, fc2 = wu_ref.shape
    fc = fc2 // 2
    i = pl.program_id(1)
    h_ref[...] = _ada_norm(x_ref[...], g_ref[...], mod_ref[4:5, :], mod_ref[3:4, :]).astype(BF16)
    acc_ref[...] = jnp.zeros_like(acc_ref)

    @pl.when(i == 0)
    def _():
        carry_ref[...] = jnp.zeros_like(carry_ref)

    def up_proj(c, ext_ref):
        up = jnp.dot(h_ref[...], wu_ref[c], preferred_element_type=F32)
        ext_ref[0:SUBLANES, :] = carry_ref[c]
        ext_ref[SUBLANES:, :] = up
        carry_ref[c] = up[tm - SUBLANES:, :]

    def down_proj(c, ext_ref):
        cw = cw_ref[c]
        out = cb_ref[c] + ext_ref[SUBLANES:, :] * cw[2:3, :]
        out = out + ext_ref[SUBLANES - 1:SUBLANES - 1 + tm, :] * cw[1:2, :]
        out = out + ext_ref[SUBLANES - 2:SUBLANES - 2 + tm, :] * cw[0:1, :]
        gate = out[:, 0:fc]
        act = (gate * _sigmoid(gate) * out[:, fc:]).astype(BF16)
        acc_ref[...] += jnp.dot(act, wd_ref[c], preferred_element_type=F32)

    assert nc % 2 == 1
    up_proj(0, exta_ref)

    def pair(jj, carry):
        c = 2 * jj
        up_proj(c + 1, extb_ref)
        down_proj(c, exta_ref)
        up_proj(c + 2, exta_ref)
        down_proj(c + 1, extb_ref)
        return carry

    lax.fori_loop(0, nc // 2, pair, 0)
    down_proj(nc - 1, exta_ref)
    y = x_ref[...] + mod_ref[5:6, :] * acc_ref[...]
    if final:
        r = lax.rsqrt(jnp.mean(y * y, axis=-1, keepdims=True) + EPS)
        y = y * r * fg_ref[...]
    y_ref[...] = y


def _ffn(x, mod, ln_g, w_up_c, cw_c, cb_c, w_down_c, final_g, tm, final):
    B, S, D = x.shape
    nc, _, fc2 = w_up_c.shape
    tok = pl.BlockSpec((None, tm, D), lambda bi, i: (bi, i, 0))
    return pl.pallas_call(
        functools.partial(_ffn_kernel, final=final),
        grid=(B, S // tm),
        in_specs=[tok,
                  pl.BlockSpec((None, 6, D), lambda bi, i: (bi, 0, 0)),
                  _const_spec((1, D)),
                  _const_spec((nc, D, fc2)),
                  _const_spec((nc, FFN_CONV_WIDTH, fc2)), _const_spec((nc, 1, fc2)),
                  _const_spec((nc, fc2 // 2, D)),
                  _const_spec((1, D))],
        out_specs=tok,
        out_shape=jax.ShapeDtypeStruct((B, S, D), F32),
        scratch_shapes=[pltpu.VMEM((tm, D), BF16),
                        pltpu.VMEM((SUBLANES + tm, fc2), F32), pltpu.VMEM((SUBLANES + tm, fc2), F32),
                        pltpu.VMEM((nc, SUBLANES, fc2), F32), pltpu.VMEM((tm, D), F32)],
        compiler_params=_params(("parallel", "arbitrary")),
        name="ffn",
    )(x, mod, ln_g, w_up_c, cw_c, cb_c, w_down_c, final_g)


def _chunk_cols(a, fc):
    lead = a.shape[:-1]
    g = a[..., :FFN_DIM].reshape(*lead, FFN_DIM // fc, fc)
    v = a[..., FFN_DIM:].reshape(*lead, FFN_DIM // fc, fc)
    return jnp.moveaxis(jnp.concatenate([g, v], axis=-1), -2, 0)


def _tile(s, pref):
    return pref if s % pref == 0 else s


def kernel(x, c, positions, ln1_g, ln2_g, w_ada, b_ada, w_in, b_in, lambda_q1, lambda_k1, lambda_q2, lambda_k2,
           attn_subln_g, w_attn_out, conv_dw_w, conv_dw_b, conv_ln_g, conv_ln_b, w_conv_out, gmlp_ln_g, gmlp_ln_b,
           w_spatial, b_spatial, w_gmlp_out, w_o, w_up, ffn_dw_w, ffn_dw_b, w_down, final_g):
    B, S, D = x.shape
    L = w_in.shape[0]
    tm = _tile(S, TOKEN_TILE)
    ta = tm

    inv_freq = 1.0 / (ROPE_THETA ** (jnp.arange(0, ATT_HEAD_DIM, 2, dtype=F32) / ATT_HEAD_DIM))
    ang = positions.astype(F32)[..., None] * inv_freq
    cos, sin = jnp.cos(ang), jnp.sin(ang)
    cos_t = jnp.concatenate([cos, cos, cos, cos], axis=-1)
    sin_t = jnp.concatenate([-sin, sin, -sin, sin], axis=-1)

    mod = _modulation(c, w_ada, b_ada).reshape(L, B, 6, D)
    row = lambda a: a.reshape(1, -1)

    nc = FFN_DIM // FFN_CHUNK
    w_branch = w_in[:, :, :BRANCH_COLS].astype(BF16)
    b_branch = b_in[:, :BRANCH_COLS]
    w_vt = jnp.swapaxes(w_in[:, :, 2 * QK_COLS:QKV_COLS], 1, 2).astype(BF16)
    b_vt = b_in[:, 2 * QK_COLS:QKV_COLS, None]
    w_gate = (0.5 * w_in[:, :, BRANCH_COLS:]).astype(BF16)
    b_gate = 0.5 * b_in[:, BRANCH_COLS:]
    b_sp = jnp.broadcast_to(b_spatial[:, :, :, None], w_spatial.shape)
    w_att = (0.5 * w_attn_out).astype(BF16)
    w_conv = (0.5 * w_conv_out).astype(BF16)
    w_gmlp = (0.5 * w_gmlp_out).astype(BF16)
    w_o_b = w_o.astype(BF16)
    conv_w = jnp.broadcast_to(conv_dw_w[:, :, None, :], (L, CONV_WIDTH, SUBLANES, CONV_CH))
    chunked = jax.vmap(lambda a: _chunk_cols(a, FFN_CHUNK))
    w_up_c = chunked(w_up.astype(BF16))
    ffn_w_c = chunked(ffn_dw_w)
    ffn_b_c = chunked(ffn_dw_b[:, None, :])
    w_down_c = w_down.astype(BF16).reshape(L, nc, FFN_CHUNK, D)

    for l in range(L):
        lam_init = 0.8 - 0.6 * math.exp(-0.3 * l)
        q, k, vt, hc, tg = _proj(x, mod[l], row(ln1_g[l]), w_branch[l], row(b_branch[l]), w_vt[l], b_vt[l],
                                cos_t, sin_t, w_spatial[l], b_sp[l], row(gmlp_ln_g[l]), row(gmlp_ln_b[l]), tm)
        o = _attention(q, k, vt, row(lambda_q1[l]), row(lambda_k1[l]), row(lambda_q2[l]), row(lambda_k2[l]),
                       row(attn_subln_g[l]), lam_init, ta)
        x = _merge(x, mod[l], row(ln1_g[l]), w_gate[l], row(b_gate[l]), o, w_att[l], hc,
                   conv_w[l], row(conv_dw_b[l]), row(conv_ln_g[l]), row(conv_ln_b[l]), w_conv[l],
                   tg, w_gmlp[l], w_o_b[l], tm)
        x = _ffn(x, mod[l], row(ln2_g[l]), w_up_c[l], ffn_w_c[l], ffn_b_c[l], w_down_c[l], row(final_g),
                 tm, l == L - 1)
    return x
```

```python
import functools
import math

import jax
import jax.numpy as jnp
from jax import lax
from jax.experimental import pallas as pl
from jax.experimental.pallas import tpu as pltpu

D_MODEL = 1024
CHUNK = 64
ATT_HEADS = 4
ATT_HEAD_DIM = 64
ATT_V_DIM = 128
ATT_WIDTH = 512
QK_COLS = 512
ROPE_THETA = 10000.0
CONV_CH = 512
CONV_WIDTH = 31
GMLP_CH = 512
GMLP_GROUPS = 4
GMLP_BLOCK = 128
FFN_DIM = 2816
FFN_CONV_WIDTH = 3
EPS = 1e-6
QKV_COLS = 2 * QK_COLS + ATT_WIDTH
CONV_IN_COLS = 2 * CONV_CH
GMLP_IN_COLS = 2 * GMLP_CH
BRANCH_COLS = QKV_COLS + CONV_IN_COLS + GMLP_IN_COLS
GATE_COLS = 3 * D_MODEL

LANES = 128
SUBLANES = 8
CONV_HALO = 32
VMEM_LIMIT = 56 * 1024 * 1024
NEG_INF = -1e30
CHUNK_SHIFT = CHUNK.bit_length() - 1
TOKEN_TILE = 512
SUM_ROWS = 16
FFN_CHUNK = 256
FFN_TILE = 1024

F32 = jnp.float32
BF16 = jnp.bfloat16


def _const_spec(shape):
    n = len(shape)
    return pl.BlockSpec(shape, lambda *_: (0,) * n, pipeline_mode=pl.Buffered(1))


def _params(sem):
    return pltpu.CompilerParams(dimension_semantics=sem, vmem_limit_bytes=VMEM_LIMIT)


def _sigmoid(x):
    return 0.5 * jnp.tanh(0.5 * x) + 0.5


def _ada_norm(x, g, scale, shift):
    r = lax.rsqrt(jnp.mean(x * x, axis=-1, keepdims=True) + EPS)
    return x * r * (g * (1.0 + scale)) + shift


def _mod_kernel(c_ref, w_ref, b_ref, o_ref):
    c = c_ref[...]
    ca = (c * _sigmoid(c)).astype(BF16)
    o_ref[...] = jnp.dot(ca, w_ref[...].astype(BF16), preferred_element_type=F32) + b_ref[...]


def _modulation(c, w_ada, b_ada):
    L, D, N = w_ada.shape
    B = c.shape[0]
    tn = 1536
    return pl.pallas_call(
        _mod_kernel,
        grid=(L, N // tn),
        in_specs=[pl.BlockSpec((B, D), lambda l, j: (0, 0)),
                  pl.BlockSpec((None, D, tn), lambda l, j: (l, 0, j)),
                  pl.BlockSpec((None, 1, tn), lambda l, j: (l, 0, j))],
        out_specs=pl.BlockSpec((None, B, tn), lambda l, j: (l, 0, j)),
        out_shape=jax.ShapeDtypeStruct((L, B, N), F32),
        compiler_params=_params(("parallel", "parallel")),
        name="adaln_mod",
    )(c, w_ada, b_ada.reshape(L, 1, N))


def _proj_kernel(x_ref, mod_ref, g_ref, w_ref, b_ref, wt_ref, bt_ref, cos_ref, sin_ref, cost_ref, sint_ref,
                 wsp_ref, bsp_ref, lng_ref, lnb_ref, qt_ref, k_ref, vt_ref, hc_ref, tg_ref):
    tm = x_ref.shape[0]
    h = _ada_norm(x_ref[...], g_ref[...], mod_ref[1:2, :], mod_ref[0:1, :]).astype(BF16)

    qvt = lax.dot_general(wt_ref[...], h, (((1,), (1,)), ((), ())), preferred_element_type=F32) + bt_ref[...]
    vt_ref[...] = qvt[QK_COLS:, :].astype(BF16)
    cost = cost_ref[...]
    sint = sint_ref[...]
    half = ATT_HEAD_DIM // 2
    scale = ATT_HEAD_DIM ** -0.5
    for r0 in range(0, QK_COLS, ATT_HEAD_DIM):
        x1 = qvt[r0:r0 + half, :]
        x2 = qvt[r0 + half:r0 + ATT_HEAD_DIM, :]
        qt_ref[r0:r0 + half, :] = ((x1 * cost - x2 * sint) * scale).astype(BF16)
        qt_ref[r0 + half:r0 + ATT_HEAD_DIM, :] = ((x2 * cost + x1 * sint) * scale).astype(BF16)

    kk = jnp.dot(h, w_ref[:, QK_COLS:2 * QK_COLS], preferred_element_type=F32) + b_ref[:, QK_COLS:2 * QK_COLS]
    cos = cos_ref[...]
    sin = sin_ref[...]
    lane = lax.broadcasted_iota(jnp.int32, (tm, LANES), 1)
    first_half = (lane & (ATT_HEAD_DIM - 1)) < half
    for c in range(0, QK_COLS, LANES):
        g = kk[:, c:c + LANES]
        fwd = pltpu.roll(g, LANES - half, axis=1)
        bwd = pltpu.roll(g, half, axis=1)
        k_ref[:, c:c + LANES] = (g * cos + jnp.where(first_half, fwd, bwd) * sin).astype(BF16)

    c0 = QKV_COLS
    cv = jnp.dot(h, w_ref[:, c0:c0 + CONV_IN_COLS], preferred_element_type=F32) + b_ref[:, c0:c0 + CONV_IN_COLS]
    hc_ref[...] = (cv[:, 0:CONV_CH] * (jnp.tanh(cv[:, CONV_CH:]) + 1.0)).astype(BF16)

    c1 = c0 + CONV_IN_COLS
    z = jax.nn.gelu(jnp.dot(h, w_ref[:, c1:c1 + GMLP_IN_COLS], preferred_element_type=F32)
                    + b_ref[:, c1:c1 + GMLP_IN_COLS])
    u = z[:, 0:GMLP_CH]
    vv = z[:, GMLP_CH:]
    mu = jnp.mean(vv, axis=-1, keepdims=True)
    vc = vv - mu
    var = jnp.mean(vc * vc, axis=-1, keepdims=True)
    vn = (vc * lax.rsqrt(var + EPS) * lng_ref[...] + lnb_ref[...]).astype(BF16)
    row = lax.broadcasted_iota(jnp.int32, (GMLP_BLOCK, GMLP_BLOCK), 0)
    col = lax.broadcasted_iota(jnp.int32, (GMLP_BLOCK, GMLP_BLOCK), 1)
    gc = GMLP_CH // GMLP_GROUPS
    for g in range(GMLP_GROUPS):
        ws = jnp.where(row >= col, wsp_ref[g], 0.0).astype(BF16)
        bias = bsp_ref[g]
        cs = slice(g * gc, (g + 1) * gc)
        nb = tm // GMLP_BLOCK
        blocks = jnp.concatenate([vn[n * GMLP_BLOCK:(n + 1) * GMLP_BLOCK, cs] for n in range(nb)], axis=1)
        sg = jnp.dot(ws, blocks, preferred_element_type=F32)
        for n in range(nb):
            rs = slice(n * GMLP_BLOCK, (n + 1) * GMLP_BLOCK)
            tg_ref[rs, cs] = (u[rs, cs] * (sg[:, n * gc:(n + 1) * gc] + bias)).astype(BF16)


def _proj(x, mod, ln_g, w, b, wt, bt, cos_t, sin_t, cos_tt, sin_tt, w_sp, b_sp_t, gln_g, gln_b, tm):
    B, S, D = x.shape
    tok = lambda n: pl.BlockSpec((None, tm, n), lambda bi, i: (bi, i, 0))
    tok_out = jax.ShapeDtypeStruct((B, S, 512), BF16)
    t_out = jax.ShapeDtypeStruct((B, S // tm, ATT_WIDTH, tm), BF16)
    t_spec = pl.BlockSpec((None, None, ATT_WIDTH, tm), lambda bi, i: (bi, i, 0, 0))
    rope_t = pl.BlockSpec((None, ATT_HEAD_DIM // 2, tm), lambda bi, i: (bi, 0, i))
    return pl.pallas_call(
        _proj_kernel,
        grid=(B, S // tm),
        in_specs=[tok(D),
                  pl.BlockSpec((None, 6, D), lambda bi, i: (bi, 0, 0)),
                  _const_spec((1, D)),
                  _const_spec((D, BRANCH_COLS)),
                  _const_spec((1, BRANCH_COLS)),
                  _const_spec((QK_COLS + ATT_WIDTH, D)), _const_spec((QK_COLS + ATT_WIDTH, 1)),
                  tok(LANES), tok(LANES), rope_t, rope_t,
                  _const_spec((GMLP_GROUPS, GMLP_BLOCK, GMLP_BLOCK)),
                  _const_spec((GMLP_GROUPS, GMLP_BLOCK, GMLP_BLOCK)),
                  _const_spec((1, GMLP_CH)), _const_spec((1, GMLP_CH))],
        out_specs=[t_spec, tok(512), t_spec, tok(512), tok(512)],
        out_shape=[t_out, tok_out, t_out, tok_out, tok_out],
        compiler_params=_params(("parallel", "parallel")),
        name="in_proj",
    )(x, mod, ln_g, w, b, wt, bt, cos_t, sin_t, cos_tt, sin_tt, w_sp, b_sp_t, gln_g, gln_b)


def _col_max(s):
    r = s.shape[0]
    while r > SUBLANES and r % 2 == 0:
        r //= 2
        s = jnp.maximum(s[:r], s[r:])
    return jnp.max(s, axis=0, keepdims=True)


def _attn_scores(qt_ref, k_ref, i, j, s_ref, mx_ref, t):
    qt = qt_ref[i]
    row = lax.broadcasted_iota(jnp.int32, qt.shape, 0)
    zero = jnp.zeros_like(qt)
    qs = jnp.concatenate([jnp.where(row < ATT_HEAD_DIM, qt, zero),
                          jnp.where(row >= ATT_HEAD_DIM, qt, zero)], axis=1)
    kj = k_ref[pl.ds(pl.multiple_of(j * t, t), t), :]
    s = jnp.dot(kj, qs, preferred_element_type=F32)
    s_ref[...] = s
    mx_ref[...] = _col_max(s)


def _attn_q_tile(i, lq1_ref, lk1_ref, lq2_ref, lk2_ref, qt_ref, k_ref, vt_ref, g_ref, o_ref,
                 m_ref, acc_ref, sa_ref, sb_ref, mxa_ref, mxb_ref, *, t, lam_init):
    nq = qt_ref.shape[0]
    m_ref[...] = jnp.full_like(m_ref, NEG_INF)
    acc_ref[...] = jnp.zeros_like(acc_ref)
    ones = jnp.ones((SUM_ROWS, t), BF16)

    def scores(j, s_ref, mx_ref):
        _attn_scores(qt_ref, k_ref, i, j, s_ref, mx_ref, t)

    def accumulate(j, s_ref, mx_ref, masked):
        s = s_ref[...]
        if masked:
            ck = lax.broadcasted_iota(jnp.int32, (t, t), 0)
            rq = lax.broadcasted_iota(jnp.int32, (t, t), 1)
            keep = (rq >> CHUNK_SHIFT) >= (ck >> CHUNK_SHIFT)
            s = jnp.where(jnp.concatenate([keep, keep], axis=1), s, NEG_INF)
            tile_max = _col_max(s)
        else:
            tile_max = mx_ref[...]
        m_old = m_ref[...]
        m_new = jnp.maximum(m_old, tile_max)
        p = jnp.exp(s - m_new).astype(BF16)
        v1 = jnp.concatenate([vt_ref[j], ones], axis=0)
        acc_ref[...] = jnp.exp(m_old - m_new) * acc_ref[...] + jnp.dot(v1, p, preferred_element_type=F32)
        m_ref[...] = m_new

    a, b = (sa_ref, mxa_ref), (sb_ref, mxb_ref)

    def body(jj, carry):
        j = 2 * jj
        scores(j + 1, *b)
        accumulate(j, *a, False)
        scores(j + 2, *a)
        accumulate(j + 1, *b, False)
        return carry

    lax.fori_loop(0, i // 2, body, 0)

    @pl.when(i % 2 == 1)
    def _():
        scores(i, *b)
        accumulate(i - 1, *a, False)
        accumulate(i, *b, True)

    @pl.when(i % 2 == 0)
    def _():
        accumulate(i, *a, True)

    _attn_scores(qt_ref, k_ref, jnp.minimum(i + 1, nq - 1), 0, *a, t)

    lam = (jnp.exp(jnp.sum(lq1_ref[...] * lk1_ref[...], axis=-1, keepdims=True))
           - jnp.exp(jnp.sum(lq2_ref[...] * lk2_ref[...], axis=-1, keepdims=True)) + lam_init)
    o_all = acc_ref[0:ATT_V_DIM, :] / acc_ref[ATT_V_DIM:ATT_V_DIM + 1, :]
    o = (o_all[:, 0:t] - lam * o_all[:, t:2 * t]).T
    r = lax.rsqrt(jnp.mean(o * o, axis=-1, keepdims=True) + EPS)
    o_ref[pl.ds(pl.multiple_of(i * t, t), t), :] = (o * r * g_ref[...] * (1.0 - lam_init)).astype(BF16)


def _attn_kernel(lq1_ref, lk1_ref, lq2_ref, lk2_ref, qt_ref, k_ref, vt_ref, g_ref, o_ref,
                 m_ref, acc_ref, sa_ref, sb_ref, mxa_ref, mxb_ref, *, t, lam_init):
    _attn_scores(qt_ref, k_ref, 0, 0, sa_ref, mxa_ref, t)

    def q_tile(i, carry):
        _attn_q_tile(i, lq1_ref, lk1_ref, lq2_ref, lk2_ref, qt_ref, k_ref, vt_ref, g_ref, o_ref,
                     m_ref, acc_ref, sa_ref, sb_ref, mxa_ref, mxb_ref, t=t, lam_init=lam_init)
        return carry

    lax.fori_loop(0, qt_ref.shape[0], q_tile, 0)


def _attention(qt, k, vt, lq1, lk1, lq2, lk2, subln_g, lam_init, t):
    B, S, _ = k.shape
    lam_spec = _const_spec((1, ATT_HEAD_DIM))
    ko_spec = pl.BlockSpec((None, S, ATT_V_DIM), lambda b, h: (b, 0, h))
    t_spec = pl.BlockSpec((None, S // t, ATT_V_DIM, t), lambda b, h: (b, 0, h, 0))
    return pl.pallas_call(
        functools.partial(_attn_kernel, t=t, lam_init=lam_init),
        grid=(B, ATT_HEADS),
        in_specs=[lam_spec, lam_spec, lam_spec, lam_spec, t_spec, ko_spec, t_spec,
                  _const_spec((1, ATT_V_DIM))],
        out_specs=ko_spec,
        out_shape=jax.ShapeDtypeStruct((B, S, ATT_WIDTH), BF16),
        scratch_shapes=[pltpu.VMEM((1, 2 * t), F32), pltpu.VMEM((ATT_V_DIM + SUM_ROWS, 2 * t), F32),
                        pltpu.VMEM((t, 2 * t), F32), pltpu.VMEM((t, 2 * t), F32),
                        pltpu.VMEM((1, 2 * t), F32), pltpu.VMEM((1, 2 * t), F32)],
        compiler_params=_params(("parallel", "parallel")),
        name="diff_attn",
    )(lq1, lk1, lq2, lk2, qt, k, vt, subln_g)


def _merge_kernel(x_ref, mod_ref, g_ref, wg_ref, bg_ref, o_ref, wa_ref, hcp_ref, hc_ref, cw_ref, cb_ref,
                  clg_ref, clb_ref, wc_ref, tg_ref, wm_ref, wo_ref, y_ref, ext_ref, cn_ref):
    tm = x_ref.shape[0]
    i = pl.program_id(1)
    x = x_ref[...]
    h = _ada_norm(x, g_ref[...], mod_ref[1:2, :], mod_ref[0:1, :]).astype(BF16)

    def gated(branch, y_half):
        cols = slice(branch * D_MODEL, (branch + 1) * D_MODEL)
        z_half = jnp.dot(h, wg_ref[:, cols], preferred_element_type=F32) + bg_ref[:, cols]
        return (jnp.tanh(z_half) + 1.0) * y_half

    mixed = gated(0, jnp.dot(o_ref[...], wa_ref[...], preferred_element_type=F32))
    mixed = mixed + gated(2, jnp.dot(tg_ref[...], wm_ref[...], preferred_element_type=F32))

    prev = hcp_ref[...].astype(F32)
    ext_ref[0, 0:CONV_HALO, :] = jnp.where(i > 0, prev, jnp.zeros_like(prev))
    ext_ref[0, CONV_HALO:, :] = hc_ref[...].astype(F32)
    n_sh = tm + CONV_HALO - SUBLANES
    for s in range(1, SUBLANES):
        ext_ref[s, 0:n_sh, :] = ext_ref[0, s:s + n_sh, :]
    rb = 32
    first = CONV_HALO - (CONV_WIDTH - 1)

    for base in range(0, tm, rb):
        acc = jnp.zeros((rb, CONV_CH), F32) + cb_ref[...]
        for j in range(CONV_WIDTH):
            s = (first + j) % SUBLANES
            r0 = base + (first + j - s)
            acc = acc + ext_ref[s, r0:r0 + rb, :] * jnp.tile(cw_ref[j], (rb // SUBLANES, 1))
        mu = jnp.mean(acc, axis=-1, keepdims=True)
        ac = acc - mu
        var = jnp.mean(ac * ac, axis=-1, keepdims=True)
        yn = ac * lax.rsqrt(var + EPS) * clg_ref[...] + clb_ref[...]
        cn_ref[base:base + rb, :] = (yn * (jnp.tanh(yn) + 1.0)).astype(BF16)

    mixed = mixed + gated(1, jnp.dot(cn_ref[...], wc_ref[...], preferred_element_type=F32))
    y_ref[...] = x + mod_ref[2:3, :] * jnp.dot(mixed.astype(BF16), wo_ref[...], preferred_element_type=F32)


def _merge(x, mod, ln_g, w_gate, b_gate, o, w_att, hc, cw, cb, clg, clb, w_conv, tg, w_gmlp, w_o, tm):
    B, S, D = x.shape
    tok = lambda n: pl.BlockSpec((None, tm, n), lambda bi, i: (bi, i, 0))
    halo = pl.BlockSpec((None, CONV_HALO, CONV_CH),
                        lambda bi, i: (bi, jnp.maximum(i * (tm // CONV_HALO) - 1, 0), 0))
    return pl.pallas_call(
        _merge_kernel,
        grid=(B, S // tm),
        in_specs=[tok(D),
                  pl.BlockSpec((None, 6, D), lambda bi, i: (bi, 0, 0)),
                  _const_spec((1, D)),
                  _const_spec((D, GATE_COLS)), _const_spec((1, GATE_COLS)),
                  tok(ATT_WIDTH), _const_spec((ATT_WIDTH, D)),
                  halo, tok(CONV_CH),
                  _const_spec((CONV_WIDTH, SUBLANES, CONV_CH)), _const_spec((1, CONV_CH)),
                  _const_spec((1, CONV_CH)), _const_spec((1, CONV_CH)),
                  _const_spec((CONV_CH, D)),
                  tok(GMLP_CH), _const_spec((GMLP_CH, D)),
                  _const_spec((D, D))],
        out_specs=tok(D),
        out_shape=jax.ShapeDtypeStruct((B, S, D), F32),
        scratch_shapes=[pltpu.VMEM((SUBLANES, CONV_HALO + tm, CONV_CH), F32), pltpu.VMEM((tm, CONV_CH), BF16)],
        compiler_params=_params(("parallel", "parallel")),
        name="merge",
    )(x, mod, ln_g, w_gate, b_gate, o, w_att, hc, hc, cw, cb, clg, clb, w_conv, tg, w_gmlp, w_o)


def _ffn_kernel(x_ref, mod_ref, g_ref, wu_ref, cw_ref, cb_ref, wd_ref, fg_ref, y_ref,
                h_ref, exta_ref, extb_ref, carry_ref, acc_ref, *, final):
    tm = x_ref.shape[0]
    nc, _, fc2 = wu_ref.shape
    fc = fc2 // 2
    i = pl.program_id(1)
    h_ref[...] = _ada_norm(x_ref[...], g_ref[...], mod_ref[4:5, :], mod_ref[3:4, :]).astype(BF16)
    acc_ref[...] = jnp.zeros_like(acc_ref)

    @pl.when(i == 0)
    def _():
        carry_ref[...] = jnp.zeros_like(carry_ref)

    def up_proj(c, ext_ref):
        up = jnp.dot(h_ref[...], wu_ref[c], preferred_element_type=F32)
        ext_ref[0:SUBLANES, :] = carry_ref[c]
        ext_ref[SUBLANES:, :] = up
        carry_ref[c] = up[tm - SUBLANES:, :]

    def down_proj(c, ext_ref):
        cw = cw_ref[c]
        out = cb_ref[c] + ext_ref[SUBLANES:, :] * cw[2:3, :]
        out = out + ext_ref[SUBLANES - 1:SUBLANES - 1 + tm, :] * cw[1:2, :]
        out = out + ext_ref[SUBLANES - 2:SUBLANES - 2 + tm, :] * cw[0:1, :]
        gate = out[:, 0:fc]
        act = (gate * (jnp.tanh(gate) + 1.0) * out[:, fc:]).astype(BF16)
        acc_ref[...] += jnp.dot(act, wd_ref[c], preferred_element_type=F32)

    up_proj(0, exta_ref)

    def quad(jj, carry):
        c = 4 * jj
        up_proj(c + 1, extb_ref)
        down_proj(c, exta_ref)
        up_proj(c + 2, exta_ref)
        down_proj(c + 1, extb_ref)
        up_proj(c + 3, extb_ref)
        down_proj(c + 2, exta_ref)
        up_proj(c + 4, exta_ref)
        down_proj(c + 3, extb_ref)
        return carry

    n_quad = (nc - 1) // 4
    lax.fori_loop(0, n_quad, quad, 0)
    bufs = (exta_ref, extb_ref)
    for c in range(4 * n_quad, nc):
        if c + 1 < nc:
            up_proj(c + 1, bufs[(c + 1) % 2])
        down_proj(c, bufs[c % 2])
    y = x_ref[...] + mod_ref[5:6, :] * acc_ref[...]
    if final:
        r = lax.rsqrt(jnp.mean(y * y, axis=-1, keepdims=True) + EPS)
        y = y * r * fg_ref[...]
    y_ref[...] = y


def _ffn(x, mod, ln_g, w_up_c, cw_c, cb_c, w_down_c, final_g, tm, final):
    B, S, D = x.shape
    nc, _, fc2 = w_up_c.shape
    tok = pl.BlockSpec((None, tm, D), lambda bi, i: (bi, i, 0))
    return pl.pallas_call(
        functools.partial(_ffn_kernel, final=final),
        grid=(B, S // tm),
        in_specs=[tok,
                  pl.BlockSpec((None, 6, D), lambda bi, i: (bi, 0, 0)),
                  _const_spec((1, D)),
                  _const_spec((nc, D, fc2)),
                  _const_spec((nc, FFN_CONV_WIDTH, fc2)), _const_spec((nc, 1, fc2)),
                  _const_spec((nc, fc2 // 2, D)),
                  _const_spec((1, D))],
        out_specs=tok,
        out_shape=jax.ShapeDtypeStruct((B, S, D), F32),
        scratch_shapes=[pltpu.VMEM((tm, D), BF16),
                        pltpu.VMEM((SUBLANES + tm, fc2), F32), pltpu.VMEM((SUBLANES + tm, fc2), F32),
                        pltpu.VMEM((nc, SUBLANES, fc2), F32), pltpu.VMEM((tm, D), F32)],
        compiler_params=_params(("parallel", "arbitrary")),
        name="ffn",
    )(x, mod, ln_g, w_up_c, cw_c, cb_c, w_down_c, final_g)


def _chunk_cols(a, fc):
    lead = a.shape[:-1]
    g = a[..., :FFN_DIM].reshape(*lead, FFN_DIM // fc, fc)
    v = a[..., FFN_DIM:].reshape(*lead, FFN_DIM // fc, fc)
    return jnp.moveaxis(jnp.concatenate([g, v], axis=-1), -2, 0)


def _tile(s, pref):
    return pref if s % pref == 0 else s


def kernel(x, c, positions, ln1_g, ln2_g, w_ada, b_ada, w_in, b_in, lambda_q1, lambda_k1, lambda_q2, lambda_k2,
           attn_subln_g, w_attn_out, conv_dw_w, conv_dw_b, conv_ln_g, conv_ln_b, w_conv_out, gmlp_ln_g, gmlp_ln_b,
           w_spatial, b_spatial, w_gmlp_out, w_o, w_up, ffn_dw_w, ffn_dw_b, w_down, final_g):
    B, S, D = x.shape
    L = w_in.shape[0]
    tm = _tile(S, TOKEN_TILE)
    ta = tm

    inv_freq = 1.0 / (ROPE_THETA ** (jnp.arange(0, ATT_HEAD_DIM, 2, dtype=F32) / ATT_HEAD_DIM))
    ang = positions.astype(F32)[..., None] * inv_freq
    cos, sin = jnp.cos(ang), jnp.sin(ang)
    cos_t = jnp.concatenate([cos, cos, cos, cos], axis=-1)
    sin_t = jnp.concatenate([-sin, sin, -sin, sin], axis=-1)
    cos_tt, sin_tt = jnp.swapaxes(cos, 1, 2), jnp.swapaxes(sin, 1, 2)

    mod = _modulation(c, w_ada, b_ada).reshape(L, B, 6, D)
    row = lambda a: a.reshape(1, -1)

    nc = FFN_DIM // FFN_CHUNK
    glu_scale = jnp.ones((BRANCH_COLS,), F32).at[QKV_COLS:QKV_COLS + CONV_IN_COLS].set(0.5)
    w_branch = (w_in[:, :, :BRANCH_COLS] * glu_scale).astype(BF16)
    b_branch = b_in[:, :BRANCH_COLS] * glu_scale
    qv_cols = jnp.concatenate([w_in[:, :, 0:QK_COLS], w_in[:, :, 2 * QK_COLS:QKV_COLS]], axis=2)
    w_t = jnp.swapaxes(qv_cols, 1, 2).astype(BF16)
    b_t = jnp.concatenate([b_in[:, 0:QK_COLS], b_in[:, 2 * QK_COLS:QKV_COLS]], axis=1)[:, :, None]
    w_gate = (0.5 * w_in[:, :, BRANCH_COLS:]).astype(BF16)
    b_gate = 0.5 * b_in[:, BRANCH_COLS:]
    b_sp = jnp.broadcast_to(b_spatial[:, :, :, None], w_spatial.shape)
    w_att = (0.5 * w_attn_out).astype(BF16)
    w_conv = (0.5 * w_conv_out).astype(BF16)
    w_gmlp = (0.5 * w_gmlp_out).astype(BF16)
    w_o_b = w_o.astype(BF16)
    conv_w = jnp.broadcast_to(conv_dw_w[:, :, None, :], (L, CONV_WIDTH, SUBLANES, CONV_CH))
    chunked = jax.vmap(lambda a: _chunk_cols(a, FFN_CHUNK))
    swish_scale = jnp.ones((2 * FFN_DIM,), F32).at[:FFN_DIM].set(0.5)
    w_up_c = chunked((w_up * swish_scale).astype(BF16))
    ffn_w_c = chunked(ffn_dw_w)
    ffn_b_c = chunked((ffn_dw_b * swish_scale)[:, None, :])
    w_down_c = w_down.astype(BF16).reshape(L, nc, FFN_CHUNK, D)

    for l in range(L):
        lam_init = 0.8 - 0.6 * math.exp(-0.3 * l)
        qt, k, vt, hc, tg = _proj(x, mod[l], row(ln1_g[l]), w_branch[l], row(b_branch[l]), w_t[l], b_t[l],
                                 cos_t, sin_t, cos_tt, sin_tt, w_spatial[l], b_sp[l],
                                 row(gmlp_ln_g[l]), row(gmlp_ln_b[l]), tm)
        o = _attention(qt, k, vt, row(lambda_q1[l]), row(lambda_k1[l]), row(lambda_q2[l]), row(lambda_k2[l]),
                       row(attn_subln_g[l]), lam_init, ta)
        x = _merge(x, mod[l], row(ln1_g[l]), w_gate[l], row(b_gate[l]), o, w_att[l], hc,
                   conv_w[l], row(conv_dw_b[l]), row(0.5 * conv_ln_g[l]), row(0.5 * conv_ln_b[l]), w_conv[l],
                   tg, w_gmlp[l], w_o_b[l], tm)
        x = _ffn(x, mod[l], row(ln2_g[l]), w_up_c[l], ffn_w_c[l], ffn_b_c[l], w_down_c[l], row(final_g),
                 _tile(S, FFN_TILE), l == L - 1)
    return x
```
